```python
import math
import jax, jax.numpy as jnp
from jax import lax
import numpy as np

D_MODEL = 1024
BATCH = 8
SEQ = 4096
DEPTH = 2
DEC_BATCH = 2
DEC_SEQ = 8192
PAST_LEN = 128

GRID_W = 64
HEAD_DIM = 64
RW_HEADS = D_MODEL // 256
AT_HEADS = D_MODEL // 128
AT_KV_HEADS = AT_HEADS // 4
AT_GROUP = AT_HEADS // AT_KV_HEADS
ML_HEADS = D_MODEL // 256
RW_W = RW_HEADS * HEAD_DIM
AT_W = AT_HEADS * HEAD_DIM
AT_KV_W = AT_KV_HEADS * HEAD_DIM
ML_W = ML_HEADS * HEAD_DIM
MIX_W = RW_W + AT_W + ML_W
DECAY_LORA = 64
AAA_LORA = 64
GATE_LORA = 128
D_FF = 4 * D_MODEL
Q_BLOCK = 128
ML_CHUNK = 64
ROPE_THETA = 10000.0
NORM_EPS = 1e-6
RW_LN_EPS = 64e-5
RW_COLS = 3 * RW_W + 2 * DECAY_LORA + 2 * AAA_LORA + GATE_LORA
AT_COLS = AT_W + 2 * AT_KV_W
ML_COLS = 4 * ML_W + 4 * ML_HEADS
N_IN = RW_COLS + AT_COLS + ML_COLS

kernel_name = "hybrid_rwkv7_gqa_mlstm_encoder"


def split_cols(t, sizes):
    idx = np.cumsum(sizes)[:-1].tolist()
    return jnp.split(t, idx, axis=-1)


def rmsnorm(x, g):
    xf = x.astype(jnp.float32)
    y = xf * lax.rsqrt(jnp.mean(xf * xf, axis=-1, keepdims=True) + NORM_EPS)
    return (y * g.astype(jnp.float32)).astype(x.dtype)


def neighbours(y):
    prev = jnp.pad(y[:, :-1], ((0, 0), (1, 0), (0, 0)))
    nxt = jnp.pad(y[:, 1:], ((0, 0), (0, 1), (0, 0)))
    return prev, nxt


def flip_t(t):
    return jnp.flip(t, axis=1)


def rwkv_scan(r, w, k, v, kk, a):
    B, S, H, N = r.shape
    xs = tuple(jnp.moveaxis(t, 1, 0) for t in (r, w, k, v, kk, a))

    def step(st, inp):
        rt, wt, kt, vt, kkt, at = inp
        sa = jnp.einsum('bhij,bhj->bhi', st, -kkt)
        st = st * wt[:, :, None, :] + sa[..., None] * (kkt * at)[:, :, None, :] + vt[..., None] * kt[:, :, None, :]
        yt = jnp.einsum('bhij,bhj->bhi', st, rt)
        return st, yt

    s0 = jnp.zeros((B, H, N, N), jnp.float32)
    _, ys = lax.scan(step, s0, xs)
    return jnp.moveaxis(ys, 0, 1)


def rwkv_mixer(cols, w0, w2, a0, a2, g2, k_k, k_a, r_k, ln_w, ln_b):
    B, S, _ = cols.shape
    r, k, v, wd, ad, gd = split_cols(cols, [RW_W, RW_W, RW_W, 2 * DECAY_LORA, 2 * AAA_LORA, GATE_LORA])
    heads = lambda t: t.reshape(B, S, RW_HEADS, HEAD_DIM)
    kk = heads(k * k_k)
    kk = kk / jnp.maximum(jnp.sqrt(jnp.sum(kk * kk, axis=-1, keepdims=True)), 1e-12)
    g = jax.nn.sigmoid(gd) @ g2
    rh, vh = heads(r), heads(v)
    wkv = 0.0
    bonus = 0.0
    for d in range(2):
        wl = jnp.tanh(wd[..., d * DECAY_LORA:(d + 1) * DECAY_LORA]) @ w2[d]
        decay = jnp.exp(-jnp.exp(-jax.nn.softplus(-(w0[d] + wl)) - 0.5))
        a = jax.nn.sigmoid(a0[d] + ad[..., d * AAA_LORA:(d + 1) * AAA_LORA] @ a2[d])
        kd = heads(k * (1.0 + (a - 1.0) * k_a))
        seq_in = (rh, heads(decay), kd, vh, kk, heads(a))
        if d == 0:
            y_d = rwkv_scan(*seq_in)
        else:
            y_d = flip_t(rwkv_scan(*[flip_t(t) for t in seq_in]))
        wkv = wkv + y_d
        bonus = bonus + jnp.sum(rh * kd * r_k, axis=-1, keepdims=True) * vh
    mu = jnp.mean(wkv, axis=-1, keepdims=True)
    var = jnp.mean(jnp.square(wkv - mu), axis=-1, keepdims=True)
    y = ((wkv - mu) * lax.rsqrt(var + RW_LN_EPS)).reshape(B, S, RW_W) * ln_w + ln_b
    return (y + bonus.reshape(B, S, RW_W)) * g


def rope_tables(seq_len):
    rows = seq_len // GRID_W
    row_idx, col_idx = jnp.meshgrid(jnp.arange(rows), jnp.arange(GRID_W), indexing='ij')
    row = row_idx.reshape(-1).astype(jnp.float32)
    col = col_idx.reshape(-1).astype(jnp.float32)
    n_freq = HEAD_DIM // 4
    inv = ROPE_THETA ** (-jnp.arange(n_freq, dtype=jnp.float32) / n_freq)
    ang_r = row[:, None] * inv
    ang_c = col[:, None] * inv
    ang = jnp.concatenate([ang_r, ang_r, ang_c, ang_c], axis=-1)
    return jnp.cos(ang), jnp.sin(ang)


def rotate_halves(x):
    n_freq = HEAD_DIM // 4
    xs = x.reshape(x.shape[:-1] + (2, 2, n_freq))
    rot = jnp.stack([-xs[..., 1, :], xs[..., 0, :]], axis=-2)
    return rot.reshape(x.shape)


def attention_mixer(cols, q_norm, k_norm, cos, sin):
    B, S, _ = cols.shape
    q, k, v = split_cols(cols, [AT_W, AT_KV_W, AT_KV_W])
    q = rmsnorm(q.reshape(B, S, AT_KV_HEADS, AT_GROUP, HEAD_DIM), q_norm)
    k = rmsnorm(k.reshape(B, S, AT_KV_HEADS, HEAD_DIM), k_norm)
    v = v.reshape(B, S, AT_KV_HEADS, HEAD_DIM)
    q = q * cos[None, :, None, None, :] + rotate_halves(q) * sin[None, :, None, None, :]
    k = k * cos[None, :, None, :] + rotate_halves(k) * sin[None, :, None, :]
    q = q * (HEAD_DIM ** -0.5)
    nq = S // Q_BLOCK
    qb = q.reshape(B, nq, Q_BLOCK, AT_KV_HEADS, AT_GROUP, HEAD_DIM).transpose(1, 0, 2, 3, 4, 5)

    def one_block(q_blk):
        s = jnp.einsum('bqhgd,bkhd->bhgqk', q_blk, k)
        p = jax.nn.softmax(s.astype(jnp.float32), axis=-1)
        return jnp.einsum('bhgqk,bkhd->bqhgd', p, v)

    o = lax.map(one_block, qb)
    return o.transpose(1, 0, 2, 3, 4, 5).reshape(B, S, AT_W)


def mlstm_chunked(q, k, v, ig, lf):
    B, S, H, Dh = q.shape
    L = ML_CHUNK
    nc = S // L
    to_c = lambda t: t.reshape(B, nc, L, H, Dh).transpose(0, 3, 1, 2, 4)
    to_g = lambda t: t.reshape(B, nc, L, H).transpose(0, 3, 1, 2)
    q, k, v = to_c(q), to_c(k), to_c(v)
    ig, lf = to_g(ig), to_g(lf)
    b = jnp.cumsum(lf, axis=-1)
    bL = b[..., -1]
    g = bL[..., None] - b + ig
    mg = jnp.max(g, axis=-1)
    wgt = jnp.exp(g - mg[..., None])
    Kc = jnp.einsum('bhcl,bhcld,bhcle->bhcde', wgt, v, k)
    Nc = jnp.einsum('bhcl,bhcle->bhce', wgt, k)

    def step(carry, inp):
        C, n, m = carry
        bl, mgc, kc, ncc = inp
        m_new = jnp.maximum(bl + m, mgc)
        a1 = jnp.exp(bl + m - m_new)
        a2 = jnp.exp(mgc - m_new)
        C_new = a1[..., None, None] * C + a2[..., None, None] * kc
        n_new = a1[..., None] * n + a2[..., None] * ncc
        return (C_new, n_new, m_new), (C, n, m)

    xs = (jnp.moveaxis(bL, 2, 0), jnp.moveaxis(mg, 2, 0), jnp.moveaxis(Kc, 2, 0), jnp.moveaxis(Nc, 2, 0))
    init = (jnp.zeros((B, H, Dh, Dh), jnp.float32), jnp.zeros((B, H, Dh), jnp.float32), jnp.zeros((B, H), jnp.float32))
    _, (Cp, Np, Mp) = lax.scan(step, init, xs)
    Cp = jnp.moveaxis(Cp, 0, 2)
    Np = jnp.moveaxis(Np, 0, 2)
    Mp = jnp.moveaxis(Mp, 0, 2)
    lower = jnp.tril(jnp.ones((L, L), dtype=bool))
    Dm = jnp.where(lower, b[..., :, None] - b[..., None, :] + ig[..., None, :], -jnp.inf)
    inter = b + Mp[..., None]
    m_t = jnp.maximum(jnp.max(Dm, axis=-1), inter)
    P = jnp.exp(Dm - m_t[..., None]) * jnp.einsum('bhctd,bhcsd->bhcts', q, k)
    sc = jnp.exp(inter - m_t)
    num = jnp.einsum('bhcts,bhcsd->bhctd', P, v) + sc[..., None] * jnp.einsum('bhcde,bhcte->bhctd', Cp, q)
    den = jnp.sum(P, axis=-1) + sc * jnp.einsum('bhce,bhcte->bhct', Np, q)
    h = num / jnp.maximum(jnp.abs(den), jnp.exp(-m_t))[..., None]
    return h.transpose(0, 2, 3, 1, 4).reshape(B, S, H, Dh)


def mlstm_mixer(cols, conv_w, i_bias, f_bias, norm_w):
    B, S, _ = cols.shape
    qk, v, o, ig, fg = split_cols(cols, [2 * ML_W, ML_W, ML_W, 2 * ML_HEADS, 2 * ML_HEADS])
    prev, nxt = neighbours(qk)
    qk = jax.nn.silu(conv_w[0] * prev + conv_w[1] * qk + conv_w[2] * nxt)
    q, k = split_cols(qk, [ML_W, ML_W])
    heads = lambda t: t.reshape(B, S, ML_HEADS, HEAD_DIM)
    q, k, v = heads(q), heads(k) * (HEAD_DIM ** -0.5), heads(v)
    h = 0.0
    for d in range(2):
        ig_d = ig[..., d * ML_HEADS:(d + 1) * ML_HEADS] + i_bias[d]
        lf_d = jax.nn.log_sigmoid(fg[..., d * ML_HEADS:(d + 1) * ML_HEADS] + f_bias[d])
        if d == 0:
            h = h + mlstm_chunked(q, k, v, ig_d, lf_d)
        else:
            h = h + flip_t(mlstm_chunked(flip_t(q), flip_t(k), flip_t(v), flip_t(ig_d), flip_t(lf_d)))
    h = rmsnorm(h, norm_w.reshape(ML_HEADS, HEAD_DIM)).reshape(B, S, ML_W)
    return jax.nn.sigmoid(o) * h


def trunk(x, P):
    S = x.shape[1]
    cos, sin = rope_tables(S)
    for l in range(DEPTH):
        xin = rmsnorm(x, P['norm1_g'][l])
        proj = (xin @ P['w_in'][l]).astype(jnp.float32)
        rw, at, ml = split_cols(proj, [RW_COLS, AT_COLS, ML_COLS])
        prev, nxt = neighbours(rw)
        rw = rw + (0.5 * (prev + nxt) - rw) * P['rw_mu'][l]
        y_rw = rwkv_mixer(rw, P['rw_w0'][l], P['rw_w2'][l], P['rw_a0'][l], P['rw_a2'][l], P['rw_g2'][l],
                          P['rw_kk'][l], P['rw_ka'][l], P['rw_rk'][l], P['rw_lnw'][l], P['rw_lnb'][l])
        y_at = attention_mixer(at, P['at_qn'][l], P['at_kn'][l], cos, sin)
        y_ml = mlstm_mixer(ml, P['ml_conv'][l], P['ml_ib'][l], P['ml_fb'][l], P['ml_nw'][l])
        mix = jnp.concatenate([y_rw, y_at, y_ml], axis=-1).astype(x.dtype)
        h = x + mix @ P['w_out'][l]
        hn = rmsnorm(h, P['norm2_g'][l])
        x = h + jnp.square(jax.nn.relu(hn @ P['mlp_w1'][l])) @ P['mlp_w2'][l]
    return rmsnorm(x, P['final_g'])


def setup_inputs(seed: int = 0) -> dict:
    key = jax.random.key(seed)
    ks = jax.random.split(key, 32)
    f32 = jnp.float32
    nrm = lambda k, shape, s: jax.random.normal(k, shape, f32) * s
    gain = lambda k, shape: 1.0 + 0.02 * jax.random.normal(k, shape, f32)
    f_bias = jnp.broadcast_to(jnp.linspace(3.0, 6.0, ML_HEADS, dtype=f32), (DEPTH, 2, ML_HEADS))
    return {
        "x_prompt": jax.random.normal(ks[0], (BATCH, SEQ, D_MODEL), f32),
        "x_sample": jax.random.normal(ks[1], (DEC_BATCH, DEC_SEQ, D_MODEL), f32),
        "norm1_g": gain(ks[2], (DEPTH, D_MODEL)),
        "w_in": nrm(ks[3], (DEPTH, D_MODEL, N_IN), D_MODEL ** -0.5),
        "rw_mu": jax.random.uniform(ks[4], (DEPTH, RW_COLS), f32),
        "rw_w0": nrm(ks[5], (DEPTH, 2, RW_W), 1.0) + 0.5,
        "rw_w2": nrm(ks[6], (DEPTH, 2, DECAY_LORA, RW_W), 0.3 * DECAY_LORA ** -0.5),
        "rw_a0": nrm(ks[7], (DEPTH, 2, RW_W), 0.1),
        "rw_a2": nrm(ks[8], (DEPTH, 2, AAA_LORA, RW_W), 0.3 * AAA_LORA ** -0.5),
        "rw_g2": nrm(ks[9], (DEPTH, GATE_LORA, RW_W), GATE_LORA ** -0.5),
        "rw_kk": 0.85 + nrm(ks[10], (DEPTH, RW_W), 0.02),
        "rw_ka": 1.0 + nrm(ks[11], (DEPTH, RW_W), 0.02),
        "rw_rk": nrm(ks[12], (DEPTH, RW_HEADS, HEAD_DIM), 0.1),
        "rw_lnw": gain(ks[13], (DEPTH, RW_W)),
        "rw_lnb": nrm(ks[14], (DEPTH, RW_W), 0.02),
        "at_qn": gain(ks[15], (DEPTH, HEAD_DIM)),
        "at_kn": gain(ks[16], (DEPTH, HEAD_DIM)),
        "ml_conv": nrm(ks[17], (DEPTH, 3, 2 * ML_W), 3 ** -0.5),
        "ml_ib": nrm(ks[18], (DEPTH, 2, ML_HEADS), 0.1),
        "ml_fb": f_bias + nrm(ks[19], (DEPTH, 2, ML_HEADS), 0.1),
        "ml_nw": gain(ks[20], (DEPTH, ML_W)),
        "w_out": nrm(ks[21], (DEPTH, MIX_W, D_MODEL), MIX_W ** -0.5),
        "norm2_g": gain(ks[22], (DEPTH, D_MODEL)),
        "mlp_w1": nrm(ks[23], (DEPTH, D_MODEL, D_FF), D_MODEL ** -0.5),
        "mlp_w2": nrm(ks[24], (DEPTH, D_FF, D_MODEL), D_FF ** -0.5),
        "final_g": gain(ks[25], (D_MODEL,)),
    }


def reference(x_prompt, x_sample, norm1_g, w_in, rw_mu, rw_w0, rw_w2, rw_a0, rw_a2, rw_g2, rw_kk, rw_ka,
              rw_rk, rw_lnw, rw_lnb, at_qn, at_kn, ml_conv, ml_ib, ml_fb, ml_nw, w_out, norm2_g,
              mlp_w1, mlp_w2, final_g):
    P = dict(norm1_g=norm1_g, w_in=w_in, rw_mu=rw_mu, rw_w0=rw_w0, rw_w2=rw_w2, rw_a0=rw_a0, rw_a2=rw_a2,
             rw_g2=rw_g2, rw_kk=rw_kk, rw_ka=rw_ka, rw_rk=rw_rk, rw_lnw=rw_lnw, rw_lnb=rw_lnb,
             at_qn=at_qn, at_kn=at_kn, ml_conv=ml_conv, ml_ib=ml_ib, ml_fb=ml_fb, ml_nw=ml_nw,
             w_out=w_out, norm2_g=norm2_g, mlp_w1=mlp_w1, mlp_w2=mlp_w2, final_g=final_g)
    y_prompt = trunk(x_prompt, P)
    y_sample = trunk(x_sample, P)
    return (y_prompt, y_sample)
```

```python
import functools

import numpy as np
import jax
import jax.numpy as jnp
from jax import lax
from jax.experimental import pallas as pl
from jax.experimental.pallas import tpu as pltpu

F32 = jnp.float32
BF16 = jnp.bfloat16

D_MODEL = 1024
DEPTH = 2
GRID_W = 64
HEAD_DIM = 64
RW_W = 256
AT_HEADS = 8
AT_KV_HEADS = 2
AT_GROUP = 4
AT_W = 512
AT_KV_W = 128
ML_HEADS = 4
ML_W = 256
DECAY_LORA = 64
AAA_LORA = 64
GATE_LORA = 128
D_FF = 4096
ROPE_THETA = 10000.0
NORM_EPS = 1e-6
RW_LN_EPS = 64e-5
RW_COLS = 1152
AT_COLS = 768
ML_COLS = 1040
CHUNK = 64
LANE = 128
SUBLANE = 8
VMEM_LIMIT = 56 * 1024 * 1024


def _params(sem):
    return pltpu.CompilerParams(dimension_semantics=sem, vmem_limit_bytes=VMEM_LIMIT)


def _dot(a, b):
    return jnp.dot(a.astype(BF16), b.astype(BF16), preferred_element_type=F32)


def _dot_nt(a, b):
    return lax.dot_general(a.astype(BF16), b.astype(BF16), (((1,), (1,)), ((), ())), preferred_element_type=F32)


def _dot_tn(a, b):
    return lax.dot_general(a.astype(BF16), b.astype(BF16), (((0,), (0,)), ((), ())), preferred_element_type=F32)


def _split3(x):
    hi = x.astype(BF16)
    r = x - hi.astype(F32)
    mid = r.astype(BF16)
    lo = (r - mid.astype(F32)).astype(BF16)
    return hi, mid, lo


def _dot_sel(x, sel):
    hi, mid, lo = _split3(x)
    d = lambda a: jnp.dot(a, sel, preferred_element_type=F32)
    return d(hi) + d(mid) + d(lo)


def _sel_dot(sel, x):
    hi, mid, lo = _split3(x)
    d = lambda a: jnp.dot(sel, a, preferred_element_type=F32)
    return d(hi) + d(mid) + d(lo)


def _rms(x, g):
    return x * lax.rsqrt(jnp.mean(x * x, axis=-1, keepdims=True) + NORM_EPS) * g


def _softplus(z):
    return jnp.maximum(z, 0.0) + jnp.log1p(jnp.exp(-jnp.abs(z)))


def _sigmoid(z):
    return 1.0 / (1.0 + jnp.exp(-z))


def _const_spec(shape):
    nd = len(shape)
    return pl.BlockSpec(shape, lambda *_: (0,) * nd)


def _shift_rows(x, prev_row, next_row):
    n = x.shape[0]
    row = lax.broadcasted_iota(jnp.int32, (n, 1), 0)
    xp = jnp.where(row == 0, prev_row, pltpu.roll(x, 1, 0))
    xn = jnp.where(row == n - 1, next_row, pltpu.roll(x, n - 1, 0))
    return xp, xn


def _halo_specs(ts, width, n_seq_tiles, n_rows):
    per = ts // SUBLANE
    last = n_rows // SUBLANE - 1
    main = pl.BlockSpec((ts, width), lambda b, j: (b * n_seq_tiles + j, 0))
    prev = pl.BlockSpec((SUBLANE, width), lambda b, j: (jnp.maximum((b * n_seq_tiles + j) * per - 1, 0), 0))
    nxt = pl.BlockSpec((SUBLANE, width), lambda b, j: (jnp.minimum((b * n_seq_tiles + j + 1) * per, last), 0))
    return main, prev, nxt


def _inproj_kernel(x_ref, g_ref, wrw_ref, wq_ref, wkv_ref, wml_ref, wg_ref,
                   rw_ref, q_ref, kv_ref, ml_ref, gc_ref):
    xb = _rms(x_ref[...], g_ref[...]).astype(BF16)
    for w_ref, o_ref in ((wrw_ref, rw_ref), (wq_ref, q_ref), (wkv_ref, kv_ref), (wml_ref, ml_ref), (wg_ref, gc_ref)):
        o_ref[...] = jnp.dot(xb, w_ref[...], preferred_element_type=F32)


def _inproj(xf, g, wrw, wq, wkv, wml, wg, tm):
    T = xf.shape[0]
    widths = (RW_COLS, AT_HEADS * LANE, 2 * AT_KV_W, 4 * ML_W, LANE)
    return pl.pallas_call(
        _inproj_kernel,
        grid=(T // tm,),
        in_specs=[pl.BlockSpec((tm, D_MODEL), lambda i: (i, 0)), _const_spec((1, D_MODEL))]
        + [_const_spec((D_MODEL, w)) for w in widths],
        out_specs=[pl.BlockSpec((tm, w), lambda i: (i, 0)) for w in widths],
        out_shape=[jax.ShapeDtypeStruct((T, w), F32) for w in widths],
        compiler_params=_params(("parallel",)),
        name="inproj",
    )(xf, g, wrw, wq, wkv, wml, wg)


def _rwprep_kernel(x_ref, hp_ref, hn_ref, mu_ref, w0_ref, w2_ref, a0_ref, a2_ref, g2_ref, kk_ref, ka_ref, bd_ref,
                   sh_ref, df_ref, db_ref):
    j = pl.program_id(1)
    x = x_ref[...]
    prev_row = jnp.where(j == 0, 0.0, hp_ref[SUBLANE - 1:SUBLANE, :])
    next_row = jnp.where(j == pl.num_programs(1) - 1, 0.0, hn_ref[0:1, :])
    xp, xn = _shift_rows(x, prev_row, next_row)
    xs = x + (0.5 * (xp + xn) - x) * mu_ref[...]
    r, k, v = xs[:, 0:RW_W], xs[:, RW_W:2 * RW_W], xs[:, 2 * RW_W:3 * RW_W]
    wd = xs[:, 3 * RW_W:3 * RW_W + LANE]
    ad = xs[:, 3 * RW_W + LANE:3 * RW_W + 2 * LANE]
    gd = xs[:, 3 * RW_W + 2 * LANE:3 * RW_W + 3 * LANE]
    kk = k * kk_ref[...]
    ss = _dot_sel(kk * kk, bd_ref[...])
    kap = kk / jnp.maximum(jnp.sqrt(ss), 1e-12)
    sh_ref[:, 0:RW_W] = r
    sh_ref[:, RW_W:2 * RW_W] = v
    sh_ref[:, 2 * RW_W:3 * RW_W] = kap
    sh_ref[:, 3 * RW_W:4 * RW_W] = _dot(_sigmoid(gd), g2_ref[...])
    tw = jnp.tanh(wd)
    for d, o_ref in ((0, df_ref), (1, db_ref)):
        wl = _dot(tw, w2_ref[d])
        o_ref[:, 0:RW_W] = -jnp.exp(-_softplus(-(w0_ref[d:d + 1, :] + wl)) - 0.5)
        a = _sigmoid(a0_ref[d:d + 1, :] + _dot(ad, a2_ref[d]))
        o_ref[:, RW_W:2 * RW_W] = k * (1.0 + (a - 1.0) * ka_ref[...])
        o_ref[:, 2 * RW_W:3 * RW_W] = a


def _rw_prep(rw, p, B, S, ts):
    T = B * S
    n = S // ts
    main, prev, nxt = _halo_specs(ts, RW_COLS, n, T)
    out_w = (4 * RW_W, 3 * RW_W, 3 * RW_W)
    return pl.pallas_call(
        _rwprep_kernel,
        grid=(B, n),
        in_specs=[main, prev, nxt, _const_spec((1, RW_COLS)), _const_spec((2, RW_W)),
                  _const_spec((2, LANE, RW_W)), _const_spec((2, RW_W)), _const_spec((2, LANE, RW_W)),
                  _const_spec((GATE_LORA, RW_W)), _const_spec((1, RW_W)), _const_spec((1, RW_W)),
                  _const_spec((RW_W, RW_W))],
        out_specs=[pl.BlockSpec((ts, w), lambda b, j: (b * n + j, 0)) for w in out_w],
        out_shape=[jax.ShapeDtypeStruct((T, w), F32) for w in out_w],
        compiler_params=_params(("parallel", "parallel")),
        name="rw_prep",
    )(rw, rw, rw, p["mu"], p["w0"], p["w2"], p["a0"], p["a2"], p["g2"], p["kk"], p["ka"], p["bd"])


def _head_blockdiag(y):
    lane = lax.broadcasted_iota(jnp.int32, y.shape, 1) // HEAD_DIM
    return jnp.concatenate([jnp.where(lane == h, y, 0.0) for h in range(4)], axis=0)


def _hmm(x, y):
    return _dot(x, _head_blockdiag(y))


def _head_diag_blocks(f):
    lane = lax.broadcasted_iota(jnp.int32, (HEAD_DIM, 4 * HEAD_DIM), 1) // HEAD_DIM
    out = jnp.where(lane == 0, f[0:HEAD_DIM, :], 0.0)
    for h in range(1, 4):
        out = out + jnp.where(lane == h, f[h * HEAD_DIM:(h + 1) * HEAD_DIM, :], 0.0)
    return out


def _rw_chunk(sh, dd, H, rev):
    L = CHUNK
    r, v, kap = sh[:, 0:RW_W], sh[:, RW_W:2 * RW_W], sh[:, 2 * RW_W:3 * RW_W]
    lw, kd, a = dd[:, 0:RW_W], dd[:, RW_W:2 * RW_W], dd[:, 2 * RW_W:3 * RW_W]
    t_i = lax.broadcasted_iota(jnp.int32, (L, RW_W), 0)
    s_i = lax.broadcasted_iota(jnp.int32, (L, RW_W), 1) % L
    tt = lax.broadcasted_iota(jnp.int32, (L, L), 0)
    ss = lax.broadcasted_iota(jnp.int32, (L, L), 1)
    if rev:
        tri = (ss >= tt).astype(BF16)
        strict, incl = s_i > t_i, s_i >= t_i
    else:
        tri = (ss <= tt).astype(BF16)
        strict, incl = s_i < t_i, s_i <= t_i
    cs = _sel_dot(tri, lw)
    last = cs[0:1, :] if rev else cs[L - 1:L, :]
    w_in, w_inv, w_ex, w_rem = jnp.exp(cs), jnp.exp(-cs), jnp.exp(cs - lw), jnp.exp(last - cs)
    b = kap * a
    rt, kt, bt, kb = r * w_in, kd * w_inv, b * w_inv, kap * w_ex
    bh, kh = b * w_rem, kd * w_rem
    gram = _dot_nt(jnp.concatenate([kb, rt], axis=0),
                   jnp.concatenate([_head_blockdiag(bt), _head_blockdiag(kt)], axis=0))
    a_b = jnp.where(strict, gram[0:L, 0:RW_W], 0.0)
    a_k = jnp.where(strict, gram[0:L, RW_W:2 * RW_W], 0.0)
    g_b = jnp.where(incl, gram[L:2 * L, 0:RW_W], 0.0)
    g_k = jnp.where(incl, gram[L:2 * L, RW_W:2 * RW_W], 0.0)
    blk16 = (t_i // 16) == (s_i // 16)
    blk32 = (t_i // 32) == (s_i // 32)
    eye = jnp.where(t_i == s_i, 1.0, 0.0)
    n1 = jnp.where(blk16, -a_b, 0.0)
    n2 = _hmm(n1, n1)
    n4 = _hmm(n2, n2)
    n8 = _hmm(n4, n4)
    x = eye + n1 + n2 + _hmm(n1, n2)
    x = x + _hmm(x, n4)
    x = x + _hmm(x, n8)
    x = x - _hmm(_hmm(x, jnp.where(blk32 & jnp.logical_not(blk16), a_b, 0.0)), x)
    tinv = x - _hmm(_hmm(x, jnp.where(blk32, 0.0, a_b)), x)
    p1 = _hmm(tinv, kb)
    p2 = _hmm(tinv, _hmm(a_k, v))
    q = rt - _hmm(g_b, p1)
    z = _hmm(g_k, v) - _hmm(g_b, p2)
    row = lax.broadcasted_iota(jnp.int32, (HEAD_DIM, RW_W), 0)
    col = lax.broadcasted_iota(jnp.int32, (HEAD_DIM, RW_W), 1) % HEAD_DIM
    m_t = jnp.where(row == col, jnp.exp(last), 0.0) - _head_diag_blocks(_dot_tn(bh, p1))
    n_t = _head_diag_blocks(_dot_tn(jnp.concatenate([kh, -bh], axis=0), jnp.concatenate([v, p2], axis=0)))
    both = _hmm(jnp.concatenate([q, m_t], axis=0), H)
    return both[0:L] + z, both[L:2 * L] + n_t


def _rwscan_kernel(shf_ref, df_ref, shb_ref, db_ref, yf_ref, yb_ref, hf_scr, hb_scr, *, cps):
    @pl.when(pl.program_id(1) == 0)
    def _():
        hf_scr[...] = jnp.zeros_like(hf_scr)
        hb_scr[...] = jnp.zeros_like(hb_scr)

    hf, hb = hf_scr[...], hb_scr[...]
    for ci in range(cps):
        lo = ci * CHUNK
        y, hf = _rw_chunk(shf_ref[lo:lo + CHUNK, :], df_ref[lo:lo + CHUNK, :], hf, False)
        yf_ref[lo:lo + CHUNK, :] = y
        lo = (cps - 1 - ci) * CHUNK
        y, hb = _rw_chunk(shb_ref[lo:lo + CHUNK, :], db_ref[lo:lo + CHUNK, :], hb, True)
        yb_ref[lo:lo + CHUNK, :] = y
    hf_scr[...] = hf
    hb_scr[...] = hb


def _rw_scan(sh, df, db, B, S, cps):
    T = B * S
    rows = cps * CHUNK
    n = S // rows
    fwd = lambda b, c: (b * n + c, 0)
    bwd = lambda b, c: (b * n + n - 1 - c, 0)
    return pl.pallas_call(
        functools.partial(_rwscan_kernel, cps=cps),
        grid=(B, n),
        in_specs=[pl.BlockSpec((rows, 4 * RW_W), fwd), pl.BlockSpec((rows, 3 * RW_W), fwd),
                  pl.BlockSpec((rows, 4 * RW_W), bwd), pl.BlockSpec((rows, 3 * RW_W), bwd)],
        out_specs=[pl.BlockSpec((rows, RW_W), fwd), pl.BlockSpec((rows, RW_W), bwd)],
        out_shape=[jax.ShapeDtypeStruct((T, RW_W), F32)] * 2,
        scratch_shapes=[pltpu.VMEM((HEAD_DIM, RW_W), F32)] * 2,
        compiler_params=_params(("parallel", "arbitrary")),
        name="rw_scan",
    )(sh, df, sh, db)


def _rope(y, cos, sin):
    lane = lax.broadcasted_iota(jnp.int32, y.shape, 1)
    rot = jnp.where(lane % 32 < 16, -pltpu.roll(y, LANE - 16, 1), pltpu.roll(y, 16, 1))
    return y * cos + rot * sin


def _atprep_kernel(q_ref, kv_ref, cos_ref, sin_ref, qn_ref, kn_ref, qo_ref, ko_ref, vo_ref):
    cos, sin = cos_ref[...], sin_ref[...]
    for h in range(AT_HEADS):
        xh = q_ref[:, h * LANE:(h + 1) * LANE]
        ms = jnp.sum(xh * xh, axis=-1, keepdims=True) * (1.0 / HEAD_DIM)
        y = xh * lax.rsqrt(ms + NORM_EPS) * qn_ref[...]
        qo_ref[:, h * LANE:(h + 1) * LANE] = (_rope(y, cos, sin) * (HEAD_DIM ** -0.5)).astype(BF16)
    kx = kv_ref[:, 0:LANE]
    low = lax.broadcasted_iota(jnp.int32, kx.shape, 1) < HEAD_DIM
    sq = kx * kx
    ms = jnp.where(low, jnp.sum(jnp.where(low, sq, 0.0), axis=-1, keepdims=True),
                   jnp.sum(jnp.where(low, 0.0, sq), axis=-1, keepdims=True)) * (1.0 / HEAD_DIM)
    ko_ref[...] = _rope(kx * lax.rsqrt(ms + NORM_EPS) * kn_ref[...], cos, sin).astype(BF16)
    vo_ref[...] = kv_ref[:, LANE:2 * LANE].astype(BF16)


def _at_prep(q, kv, cos, sin, qn, kn, S, tm):
    T = q.shape[0]
    n = S // tm
    tok = lambda w: pl.BlockSpec((tm, w), lambda i: (i, 0))
    pos = pl.BlockSpec((tm, LANE), lambda i: (i % n, 0))
    return pl.pallas_call(
        _atprep_kernel,
        grid=(T // tm,),
        in_specs=[tok(AT_HEADS * LANE), tok(2 * AT_KV_W), pos, pos, _const_spec((1, LANE)), _const_spec((1, LANE))],
        out_specs=[tok(AT_HEADS * LANE), tok(LANE), tok(LANE)],
        out_shape=[jax.ShapeDtypeStruct((T, AT_HEADS * LANE), BF16), jax.ShapeDtypeStruct((T, LANE), BF16),
                   jax.ShapeDtypeStruct((T, LANE), BF16)],
        compiler_params=_params(("parallel",)),
        name="at_prep",
    )(q, kv, cos, sin, qn, kn)


def _attn_kernel(q_ref, k_ref, v_ref, o_ref, *, tq, tk, n_k):
    q = jnp.concatenate([q_ref[:, h * LANE:(h + 1) * LANE] for h in range(AT_HEADS)], axis=0)
    m_rows = AT_HEADS * tq

    def body(c, carry):
        m, l, acc = carry
        start = pl.multiple_of(c * tk, tk)
        s = lax.dot_general(q, k_ref[pl.ds(start, tk), :], (((1,), (1,)), ((), ())), preferred_element_type=F32)
        m_new = jnp.maximum(m, jnp.max(s, axis=-1, keepdims=True))
        p = jnp.exp(s - m_new)
        alpha = jnp.exp(m - m_new)
        l = alpha * l + jnp.sum(p, axis=-1, keepdims=True)
        acc = alpha * acc + jnp.dot(p.astype(BF16), v_ref[pl.ds(start, tk), :], preferred_element_type=F32)
        return m_new, l, acc

    init = (jnp.full((m_rows, 1), -jnp.inf, F32), jnp.zeros((m_rows, 1), F32), jnp.zeros((m_rows, LANE), F32))
    _, l, acc = lax.fori_loop(0, n_k, body, init)
    o = acc / l
    for h in range(AT_HEADS):
        o_ref[:, h * LANE:(h + 1) * LANE] = o[h * tq:(h + 1) * tq, :].astype(BF16)


def _attn(qh, kh, vh, B, S, tq, tk):
    T = B * S
    n = S // tq
    return pl.pallas_call(
        functools.partial(_attn_kernel, tq=tq, tk=tk, n_k=S // tk),
        grid=(B, n),
        in_specs=[pl.BlockSpec((tq, AT_HEADS * LANE), lambda b, i: (b * n + i, 0)),
                  pl.BlockSpec((S, LANE), lambda b, i: (b, 0)), pl.BlockSpec((S, LANE), lambda b, i: (b, 0))],
        out_specs=pl.BlockSpec((tq, AT_HEADS * LANE), lambda b, i: (b * n + i, 0)),
        out_shape=jax.ShapeDtypeStruct((T, AT_HEADS * LANE), BF16),
        compiler_params=_params(("parallel", "parallel")),
        name="attn",
    )(qh, kh, vh)


def _mlprep_kernel(x_ref, hp_ref, hn_ref, cw_ref, q_ref, k_ref):
    j = pl.program_id(1)
    x = x_ref[...]
    prev_row = jnp.where(j == 0, 0.0, hp_ref[SUBLANE - 1:SUBLANE, :])
    next_row = jnp.where(j == pl.num_programs(1) - 1, 0.0, hn_ref[0:1, :])
    xp, xn = _shift_rows(x, prev_row, next_row)
    y = cw_ref[0:1, :] * xp + cw_ref[1:2, :] * x + cw_ref[2:3, :] * xn
    y = y * _sigmoid(y)
    q_ref[...] = y[:, 0:ML_W]
    k_ref[...] = y[:, ML_W:2 * ML_W] * (HEAD_DIM ** -0.5)


def _ml_prep(ml, cw, B, S, ts):
    T = B * S
    n = S // ts
    main, prev, nxt = _halo_specs(ts, 2 * ML_W, n, T)
    return pl.pallas_call(
        _mlprep_kernel,
        grid=(B, n),
        in_specs=[main, prev, nxt, _const_spec((3, 2 * ML_W))],
        out_specs=[pl.BlockSpec((ts, ML_W), lambda b, j: (b * n + j, 0))] * 2,
        out_shape=[jax.ShapeDtypeStruct((T, ML_W), F32)] * 2,
        compiler_params=_params(("parallel", "parallel")),
        name="ml_prep",
    )(ml, ml, ml, cw)


def _ml_chunk_head(q, k, v, igc, lfc, ct, n, m, rev):
    L = CHUNK
    tt = lax.broadcasted_iota(jnp.int32, (L, L), 0)
    ss = lax.broadcasted_iota(jnp.int32, (L, L), 1)
    causal = (ss >= tt) if rev else (ss <= tt)
    diag = tt == ss
    to_row = lambda c: jnp.sum(jnp.where(diag, c, 0.0), axis=0, keepdims=True)
    lf_row, ig_row = to_row(lfc), to_row(igc)
    bcol = jnp.sum(jnp.where(causal, lf_row, 0.0), axis=1, keepdims=True)
    brow = to_row(bcol)
    b_last = jnp.sum(lfc, axis=0, keepdims=True)
    g = b_last - bcol + igc
    mg = jnp.max(g, axis=0, keepdims=True)
    wgt = jnp.exp(g - mg)
    kc = _dot_tn(k, wgt * v)
    nc = jnp.sum(wgt * k, axis=0, keepdims=True)
    dm = jnp.where(causal, bcol - brow + ig_row, -jnp.inf)
    inter = bcol + m
    m_t = jnp.maximum(jnp.max(dm, axis=1, keepdims=True), inter)
    p = jnp.exp(dm - m_t) * _dot_nt(q, k)
    sc = jnp.exp(inter - m_t)
    num = _dot(p, v) + sc * _dot(q, ct)
    den = jnp.sum(p, axis=1, keepdims=True) + sc * jnp.sum(q * n, axis=1, keepdims=True)
    h = num / jnp.maximum(jnp.abs(den), jnp.exp(-m_t))
    m_new = jnp.maximum(b_last + m, mg)
    a1 = jnp.exp(b_last + m - m_new)
    a2 = jnp.exp(mg - m_new)
    return h, a1 * ct + a2 * kc, a1 * n + a2 * nc, m_new


def _mlscan_kernel(qf_ref, kf_ref, vf_ref, gf_ref, qb_ref, kb_ref, vb_ref, gb_ref, bias_ref,
                   hf_ref, hb_ref, ct_scr, n_scr, m_scr, *, cps):
    @pl.when(pl.program_id(1) == 0)
    def _():
        ct_scr[...] = jnp.zeros_like(ct_scr)
        n_scr[...] = jnp.zeros_like(n_scr)
        m_scr[...] = jnp.zeros_like(m_scr)

    dirs = ((0, False, qf_ref, kf_ref, vf_ref, gf_ref, hf_ref), (1, True, qb_ref, kb_ref, vb_ref, gb_ref, hb_ref))
    for d, rev, q_ref, k_ref, v_ref, g_ref, h_ref in dirs:
        gates = g_ref[...] + bias_ref[...]
        lf_all = jnp.minimum(gates, 0.0) - jnp.log1p(jnp.exp(-jnp.abs(gates)))
        for h in range(ML_HEADS):
            idx = d * ML_HEADS + h
            cols = slice(h * HEAD_DIM, (h + 1) * HEAD_DIM)
            ct, n, m = ct_scr[idx], n_scr[idx], m_scr[idx]
            for ci in range(cps):
                lo = (cps - 1 - ci) * CHUNK if rev else ci * CHUNK
                rows = slice(lo, lo + CHUNK)
                out, ct, n, m = _ml_chunk_head(
                    q_ref[rows, cols], k_ref[rows, cols], v_ref[rows, cols],
                    gates[lo:lo + CHUNK, idx:idx + 1], lf_all[lo:lo + CHUNK, 8 + idx:9 + idx], ct, n, m, rev)
                h_ref[rows, cols] = out
            ct_scr[idx], n_scr[idx], m_scr[idx] = ct, n, m


def _ml_scan(mq, mk, ml, gc, bias, B, S, cps):
    T = B * S
    rows = cps * CHUNK
    n = S // rows
    fwd = lambda b, c: (b * n + c, 0)
    bwd = lambda b, c: (b * n + n - 1 - c, 0)
    fwd_v = lambda b, c: (b * n + c, 2)
    bwd_v = lambda b, c: (b * n + n - 1 - c, 2)
    tile = lambda im: pl.BlockSpec((rows, ML_W), im)
    gate = lambda im: pl.BlockSpec((rows, LANE), im)
    return pl.pallas_call(
        functools.partial(_mlscan_kernel, cps=cps),
        grid=(B, n),
        in_specs=[tile(fwd), tile(fwd), tile(fwd_v), gate(fwd), tile(bwd), tile(bwd), tile(bwd_v), gate(bwd),
                  _const_spec((1, LANE))],
        out_specs=[tile(fwd), tile(bwd)],
        out_shape=[jax.ShapeDtypeStruct((T, ML_W), F32)] * 2,
        scratch_shapes=[pltpu.VMEM((2 * ML_HEADS, HEAD_DIM, HEAD_DIM), F32),
                        pltpu.VMEM((2 * ML_HEADS, 1, HEAD_DIM), F32),
                        pltpu.VMEM((2 * ML_HEADS, 1, 1), F32)],
        compiler_params=_params(("parallel", "arbitrary")),
        name="ml_scan",
    )(mq, mk, ml, gc, mq, mk, ml, gc, bias)


def _outproj_kernel(x_ref, yf_ref, yb_ref, sh_ref, df_ref, db_ref, ao_ref, hf_ref, hb_ref, og_ref,
                    rk_ref, lnw_ref, lnb_ref, nw_ref, bd_ref, worw_ref, woat_ref, woml_ref, o_ref):
    bd = bd_ref[...]
    inv = 1.0 / HEAD_DIM
    wkv = yf_ref[...] + yb_ref[...]
    dev = wkv - _dot_sel(wkv, bd) * inv
    var = _dot_sel(dev * dev, bd) * inv
    y = dev * lax.rsqrt(var + RW_LN_EPS) * lnw_ref[...] + lnb_ref[...]
    r, v, g = sh_ref[:, 0:RW_W], sh_ref[:, RW_W:2 * RW_W], sh_ref[:, 3 * RW_W:4 * RW_W]
    bonus = _dot_sel(r * (df_ref[...] + db_ref[...]) * rk_ref[...], bd) * v
    y_rw = (y + bonus) * g
    hm = hf_ref[...] + hb_ref[...]
    hn = hm * lax.rsqrt(_dot_sel(hm * hm, bd) * inv + NORM_EPS) * nw_ref[...]
    y_ml = _sigmoid(og_ref[...]) * hn
    o_ref[...] = (x_ref[...] + _dot(y_rw, worw_ref[...])
                  + jnp.dot(ao_ref[...], woat_ref[...], preferred_element_type=F32) + _dot(y_ml, woml_ref[...]))


def _outproj(xf, yf, yb, sh, df, db, ao, hf, hb, ml, p, tm):
    T = xf.shape[0]
    tok = lambda w, c=0: pl.BlockSpec((tm, w), lambda i: (i, c))
    vec = _const_spec((1, RW_W))
    return pl.pallas_call(
        _outproj_kernel,
        grid=(T // tm,),
        in_specs=[tok(D_MODEL), tok(RW_W), tok(RW_W), tok(4 * RW_W), tok(RW_W, 1), tok(RW_W, 1),
                  tok(AT_HEADS * LANE), tok(ML_W), tok(ML_W), tok(ML_W, 3),
                  vec, vec, vec, vec, _const_spec((RW_W, RW_W)),
                  _const_spec((RW_W, D_MODEL)), _const_spec((AT_HEADS * LANE, D_MODEL)), _const_spec((ML_W, D_MODEL))],
        out_specs=tok(D_MODEL),
        out_shape=jax.ShapeDtypeStruct((T, D_MODEL), F32),
        compiler_params=_params(("parallel",)),
        name="outproj",
    )(xf, yf, yb, sh, df, db, ao, hf, hb, ml, p["rk"], p["lnw"], p["lnb"], p["nw"], p["bd"],
      p["wo_rw"], p["wo_at"], p["wo_ml"])


def _mlp_kernel(h_ref, g_ref, w1_ref, w2_ref, gf_ref, o_ref, *, final):
    h = h_ref[...]
    u = jnp.dot(_rms(h, g_ref[...]).astype(BF16), w1_ref[...], preferred_element_type=F32)
    u = jnp.square(jnp.maximum(u, 0.0)).astype(BF16)
    out = h + jnp.dot(u, w2_ref[...], preferred_element_type=F32)
    if final:
        out = _rms(out, gf_ref[...])
    o_ref[...] = out


def _mlp(h, g, w1, w2, gf, final, tm):
    T = h.shape[0]
    once = pl.Buffered(1)
    return pl.pallas_call(
        functools.partial(_mlp_kernel, final=final),
        grid=(T // tm,),
        in_specs=[pl.BlockSpec((tm, D_MODEL), lambda i: (i, 0)), _const_spec((1, D_MODEL)),
                  pl.BlockSpec((D_MODEL, D_FF), lambda i: (0, 0), pipeline_mode=once),
                  pl.BlockSpec((D_FF, D_MODEL), lambda i: (0, 0), pipeline_mode=once),
                  _const_spec((1, D_MODEL))],
        out_specs=pl.BlockSpec((tm, D_MODEL), lambda i: (i, 0)),
        out_shape=jax.ShapeDtypeStruct((T, D_MODEL), F32),
        compiler_params=_params(("parallel",)),
        name="mlp_final" if final else "mlp",
    )(h, g, w1, w2, gf)


def _rope_tables(seq_len):
    t = np.arange(seq_len)
    n_freq = HEAD_DIM // 4
    inv = ROPE_THETA ** (-jnp.arange(n_freq, dtype=F32) / n_freq)
    ang_r = jnp.asarray(t // GRID_W, F32)[:, None] * inv
    ang_c = jnp.asarray(t % GRID_W, F32)[:, None] * inv
    ang = jnp.concatenate([ang_r, ang_r, ang_c, ang_c] * 2, axis=-1)
    return jnp.cos(ang), jnp.sin(ang)


def _pad_lanes_per_head(w, axis):
    parts = []
    for h in range(AT_HEADS):
        blk = lax.slice_in_dim(w, h * HEAD_DIM, (h + 1) * HEAD_DIM, axis=axis)
        zero = jnp.zeros_like(blk)
        parts += [blk, zero] if h // AT_GROUP == 0 else [zero, blk]
    return jnp.concatenate(parts, axis=axis)


def _layer_params(l, norm1_g, w_in, rw_mu, rw_w0, rw_w2, rw_a0, rw_a2, rw_g2, rw_kk, rw_ka, rw_rk, rw_lnw, rw_lnb,
                  at_qn, at_kn, ml_conv, ml_ib, ml_fb, ml_nw, w_out, norm2_g, mlp_w1, mlp_w2):
    w = w_in[l]
    w_at = w[:, RW_COLS:RW_COLS + AT_COLS]
    w_ml = w[:, RW_COLS + AT_COLS:]
    zero_lora = jnp.zeros((DECAY_LORA, RW_W), F32)
    lora = lambda m: jnp.stack([jnp.concatenate([m[0], zero_lora], 0), jnp.concatenate([zero_lora, m[1]], 0)])
    idx = np.arange(RW_W) // HEAD_DIM
    wo = w_out[l]
    return dict(
        g1=norm1_g[l][None, :],
        w_rw=w[:, :RW_COLS].astype(BF16),
        w_q=_pad_lanes_per_head(w_at[:, :AT_W], 1).astype(BF16),
        w_kv=w_at[:, AT_W:].astype(BF16),
        w_ml=w_ml[:, :4 * ML_W].astype(BF16),
        w_g=jnp.pad(w_ml[:, 4 * ML_W:], ((0, 0), (0, LANE - 4 * ML_HEADS))).astype(BF16),
        mu=rw_mu[l][None, :], w0=rw_w0[l], w2=lora(rw_w2[l]), a0=rw_a0[l], a2=lora(rw_a2[l]), g2=rw_g2[l],
        kk=rw_kk[l][None, :], ka=rw_ka[l][None, :],
        bd=jnp.asarray(idx[:, None] == idx[None, :], BF16),
        rk=rw_rk[l].reshape(1, RW_W), lnw=rw_lnw[l][None, :], lnb=rw_lnb[l][None, :],
        qn=jnp.tile(at_qn[l], 2)[None, :], kn=jnp.tile(at_kn[l], 2)[None, :],
        conv=ml_conv[l],
        gbias=jnp.pad(jnp.concatenate([ml_ib[l].reshape(-1), ml_fb[l].reshape(-1)]),
                      (0, LANE - 4 * ML_HEADS))[None, :],
        nw=ml_nw[l][None, :],
        wo_rw=wo[:RW_W].astype(BF16),
        wo_at=_pad_lanes_per_head(wo[RW_W:RW_W + AT_W], 0).astype(BF16),
        wo_ml=wo[RW_W + AT_W:].astype(BF16),
        g2n=norm2_g[l][None, :], w1=mlp_w1[l].astype(BF16), w2m=mlp_w2[l].astype(BF16),
    )


def _tiles(S):
    return dict(tm=512, ts=512, cps=2, tq=128, tk=512)


def _trunk(x, layers, final_g):
    B, S, _ = x.shape
    t = _tiles(S)
    xf = x.reshape(B * S, D_MODEL)
    cos, sin = _rope_tables(S)
    gf = final_g[None, :]
    for l, p in enumerate(layers):
        rw, q, kv, ml, gc = _inproj(xf, p["g1"], p["w_rw"], p["w_q"], p["w_kv"], p["w_ml"], p["w_g"], t["tm"])
        sh, df, db = _rw_prep(rw, p, B, S, t["ts"])
        yf, yb = _rw_scan(sh, df, db, B, S, t["cps"])
        qh, kh, vh = _at_prep(q, kv, cos, sin, p["qn"], p["kn"], S, t["tm"])
        ao = _attn(qh, kh, vh, B, S, t["tq"], t["tk"])
        mq, mk = _ml_prep(ml, p["conv"], B, S, t["ts"])
        hf, hb = _ml_scan(mq, mk, ml, gc, p["gbias"], B, S, t["cps"])
        h = _outproj(xf, yf, yb, sh, df, db, ao, hf, hb, ml, p, t["tm"])
        xf = _mlp(h, p["g2n"], p["w1"], p["w2m"], gf, l == len(layers) - 1, t["tm"])
    return xf.reshape(B, S, D_MODEL)


def kernel(x_prompt, x_sample, norm1_g, w_in, rw_mu, rw_w0, rw_w2, rw_a0, rw_a2, rw_g2, rw_kk, rw_ka, rw_rk,
           rw_lnw, rw_lnb, at_qn, at_kn, ml_conv, ml_ib, ml_fb, ml_nw, w_out, norm2_g, mlp_w1, mlp_w2, final_g):
    layers = [_layer_params(l, norm1_g, w_in, rw_mu, rw_w0, rw_w2, rw_a0, rw_a2, rw_g2, rw_kk, rw_ka, rw_rk,
                            rw_lnw, rw_lnb, at_qn, at_kn, ml_conv, ml_ib, ml_fb, ml_nw, w_out, norm2_g,
                            mlp_w1, mlp_w2) for l in range(DEPTH)]
    return _trunk(x_prompt, layers, final_g), _trunk(x_sample, layers, final_g)
```

```python
import functools

import numpy as np
import jax
import jax.numpy as jnp
from jax import lax
from jax.experimental import pallas as pl
from jax.experimental.pallas import tpu as pltpu

F32 = jnp.float32
BF16 = jnp.bfloat16

D_MODEL = 1024
DEPTH = 2
GRID_W = 64
HEAD_DIM = 64
RW_W = 256
AT_HEADS = 8
AT_KV_HEADS = 2
AT_GROUP = 4
AT_W = 512
AT_KV_W = 128
ML_HEADS = 4
ML_W = 256
DECAY_LORA = 64
AAA_LORA = 64
GATE_LORA = 128
D_FF = 4096
ROPE_THETA = 10000.0
NORM_EPS = 1e-6
RW_LN_EPS = 64e-5
RW_COLS = 1152
AT_COLS = 768
ML_COLS = 1040
CHUNK = 64
LANE = 128
SUBLANE = 8
VMEM_LIMIT = 56 * 1024 * 1024
Q_SCALE = HEAD_DIM ** -0.5 * float(np.log2(np.e))


def _params(sem):
    return pltpu.CompilerParams(dimension_semantics=sem, vmem_limit_bytes=VMEM_LIMIT)


def _dot(a, b):
    return jnp.dot(a.astype(BF16), b.astype(BF16), preferred_element_type=F32)


def _dot_nt(a, b):
    return lax.dot_general(a.astype(BF16), b.astype(BF16), (((1,), (1,)), ((), ())), preferred_element_type=F32)


def _dot_tn(a, b):
    return lax.dot_general(a.astype(BF16), b.astype(BF16), (((0,), (0,)), ((), ())), preferred_element_type=F32)


def _split3(x):
    hi = x.astype(BF16)
    r = x - hi.astype(F32)
    mid = r.astype(BF16)
    lo = (r - mid.astype(F32)).astype(BF16)
    return hi, mid, lo


def _dot_sel(x, sel):
    hi, mid, lo = _split3(x)
    d = lambda a: jnp.dot(a, sel, preferred_element_type=F32)
    return d(hi) + d(mid) + d(lo)


def _sel_dot(sel, x):
    hi, mid, lo = _split3(x)
    d = lambda a: jnp.dot(sel, a, preferred_element_type=F32)
    return d(hi) + d(mid) + d(lo)


def _rms(x, g):
    return x * lax.rsqrt(jnp.mean(x * x, axis=-1, keepdims=True) + NORM_EPS) * g


def _softplus(z):
    return jnp.maximum(z, 0.0) + jnp.log1p(jnp.exp(-jnp.abs(z)))


def _sigmoid(z):
    return 1.0 / (1.0 + jnp.exp(-z))


def _const_spec(shape):
    nd = len(shape)
    return pl.BlockSpec(shape, lambda *_: (0,) * nd)


def _shift_rows(x, prev_row, next_row):
    n = x.shape[0]
    row = lax.broadcasted_iota(jnp.int32, (n, 1), 0)
    xp = jnp.where(row == 0, prev_row, pltpu.roll(x, 1, 0))
    xn = jnp.where(row == n - 1, next_row, pltpu.roll(x, n - 1, 0))
    return xp, xn


def _halo_specs(ts, width, n_seq_tiles, n_rows):
    per = ts // SUBLANE
    last = n_rows // SUBLANE - 1
    main = pl.BlockSpec((ts, width), lambda b, j: (b * n_seq_tiles + j, 0))
    prev = pl.BlockSpec((SUBLANE, width), lambda b, j: (jnp.maximum((b * n_seq_tiles + j) * per - 1, 0), 0))
    nxt = pl.BlockSpec((SUBLANE, width), lambda b, j: (jnp.minimum((b * n_seq_tiles + j + 1) * per, last), 0))
    return main, prev, nxt


def _inproj_kernel(x_ref, g_ref, wrw_ref, wq_ref, wkv_ref, wml_ref, wg_ref,
                   rw_ref, q_ref, kv_ref, ml_ref, gc_ref):
    xb = _rms(x_ref[...], g_ref[...]).astype(BF16)
    for w_ref, o_ref in ((wrw_ref, rw_ref), (wq_ref, q_ref), (wkv_ref, kv_ref), (wml_ref, ml_ref), (wg_ref, gc_ref)):
        o_ref[...] = jnp.dot(xb, w_ref[...], preferred_element_type=F32)


def _inproj(xf, g, wrw, wq, wkv, wml, wg, tm):
    T = xf.shape[0]
    widths = (RW_COLS, AT_HEADS * LANE, 2 * AT_KV_W, 4 * ML_W, LANE)
    return pl.pallas_call(
        _inproj_kernel,
        grid=(T // tm,),
        in_specs=[pl.BlockSpec((tm, D_MODEL), lambda i: (i, 0)), _const_spec((1, D_MODEL))]
        + [_const_spec((D_MODEL, w)) for w in widths],
        out_specs=[pl.BlockSpec((tm, w), lambda i: (i, 0)) for w in widths],
        out_shape=[jax.ShapeDtypeStruct((T, w), F32) for w in widths],
        compiler_params=_params(("parallel",)),
        name="inproj",
    )(xf, g, wrw, wq, wkv, wml, wg)


def _rwprep_kernel(x_ref, hp_ref, hn_ref, mu_ref, w0_ref, w2_ref, a0_ref, a2_ref, g2_ref, kk_ref, ka_ref, bd_ref,
                   sh_ref, df_ref, db_ref):
    j = pl.program_id(1)
    x = x_ref[...]
    prev_row = jnp.where(j == 0, 0.0, hp_ref[SUBLANE - 1:SUBLANE, :])
    next_row = jnp.where(j == pl.num_programs(1) - 1, 0.0, hn_ref[0:1, :])
    xp, xn = _shift_rows(x, prev_row, next_row)
    xs = x + (0.5 * (xp + xn) - x) * mu_ref[...]
    r, k, v = xs[:, 0:RW_W], xs[:, RW_W:2 * RW_W], xs[:, 2 * RW_W:3 * RW_W]
    wd = xs[:, 3 * RW_W:3 * RW_W + LANE]
    ad = xs[:, 3 * RW_W + LANE:3 * RW_W + 2 * LANE]
    gd = xs[:, 3 * RW_W + 2 * LANE:3 * RW_W + 3 * LANE]
    kk = k * kk_ref[...]
    ss = _dot_sel(kk * kk, bd_ref[...])
    kap = kk / jnp.maximum(jnp.sqrt(ss), 1e-12)
    sh_ref[:, 0:RW_W] = r
    sh_ref[:, RW_W:2 * RW_W] = v
    sh_ref[:, 2 * RW_W:3 * RW_W] = kap
    sh_ref[:, 3 * RW_W:4 * RW_W] = _dot(_sigmoid(gd), g2_ref[...])
    tw = jnp.tanh(wd)
    for d, o_ref in ((0, df_ref), (1, db_ref)):
        wl = _dot(tw, w2_ref[d])
        o_ref[:, 0:RW_W] = -jnp.exp(-_softplus(-(w0_ref[d:d + 1, :] + wl)) - 0.5)
        a = _sigmoid(a0_ref[d:d + 1, :] + _dot(ad, a2_ref[d]))
        o_ref[:, RW_W:2 * RW_W] = k * (1.0 + (a - 1.0) * ka_ref[...])
        o_ref[:, 2 * RW_W:3 * RW_W] = a


def _rw_prep(rw, p, B, S, ts):
    T = B * S
    n = S // ts
    main, prev, nxt = _halo_specs(ts, RW_COLS, n, T)
    out_w = (4 * RW_W, 3 * RW_W, 3 * RW_W)
    return pl.pallas_call(
        _rwprep_kernel,
        grid=(B, n),
        in_specs=[main, prev, nxt, _const_spec((1, RW_COLS)), _const_spec((2, RW_W)),
                  _const_spec((2, LANE, RW_W)), _const_spec((2, RW_W)), _const_spec((2, LANE, RW_W)),
                  _const_spec((GATE_LORA, RW_W)), _const_spec((1, RW_W)), _const_spec((1, RW_W)),
                  _const_spec((RW_W, RW_W))],
        out_specs=[pl.BlockSpec((ts, w), lambda b, j: (b * n + j, 0)) for w in out_w],
        out_shape=[jax.ShapeDtypeStruct((T, w), F32) for w in out_w],
        compiler_params=_params(("parallel", "parallel")),
        name="rw_prep",
    )(rw, rw, rw, p["mu"], p["w0"], p["w2"], p["a0"], p["a2"], p["g2"], p["kk"], p["ka"], p["bd"])


def _head_blockdiag(y):
    lane = lax.broadcasted_iota(jnp.int32, y.shape, 1) // HEAD_DIM
    return jnp.concatenate([jnp.where(lane == h, y, 0.0) for h in range(4)], axis=0)


def _hmm(x, y):
    return _dot(x, _head_blockdiag(y))


def _head_diag_blocks(f):
    lane = lax.broadcasted_iota(jnp.int32, (HEAD_DIM, 4 * HEAD_DIM), 1) // HEAD_DIM
    out = jnp.where(lane == 0, f[0:HEAD_DIM, :], 0.0)
    for h in range(1, 4):
        out = out + jnp.where(lane == h, f[h * HEAD_DIM:(h + 1) * HEAD_DIM, :], 0.0)
    return out


def _in_lockstep(gens):
    out = [None] * len(gens)
    live = list(enumerate(gens))
    while live:
        still = []
        for i, g in live:
            try:
                next(g)
                still.append((i, g))
            except StopIteration as done:
                out[i] = done.value
        live = still
    return out


def _rw_chunk(sh, dd, rev):
    L = CHUNK
    r, v, kap = sh[:, 0:RW_W], sh[:, RW_W:2 * RW_W], sh[:, 2 * RW_W:3 * RW_W]
    lw, kd, a = dd[:, 0:RW_W], dd[:, RW_W:2 * RW_W], dd[:, 2 * RW_W:3 * RW_W]
    t_i = lax.broadcasted_iota(jnp.int32, (L, RW_W), 0)
    s_i = lax.broadcasted_iota(jnp.int32, (L, RW_W), 1) % L
    tt = lax.broadcasted_iota(jnp.int32, (L, L), 0)
    ss = lax.broadcasted_iota(jnp.int32, (L, L), 1)
    if rev:
        tri = (ss >= tt).astype(BF16)
        strict, incl = s_i > t_i, s_i >= t_i
    else:
        tri = (ss <= tt).astype(BF16)
        strict, incl = s_i < t_i, s_i <= t_i
    cs = _sel_dot(tri, lw)
    yield
    last = cs[0:1, :] if rev else cs[L - 1:L, :]
    w_in, w_inv, w_ex, w_rem = jnp.exp(cs), jnp.exp(-cs), jnp.exp(cs - lw), jnp.exp(last - cs)
    b = kap * a
    rt, kt, bt, kb = r * w_in, kd * w_inv, b * w_inv, kap * w_ex
    bh, kh = b * w_rem, kd * w_rem
    gram = _dot_nt(jnp.concatenate([kb, rt], axis=0),
                   jnp.concatenate([_head_blockdiag(bt), _head_blockdiag(kt)], axis=0))
    yield
    a_b = jnp.where(strict, gram[0:L, 0:RW_W], 0.0)
    a_k = jnp.where(strict, gram[0:L, RW_W:2 * RW_W], 0.0)
    g_b = jnp.where(incl, gram[L:2 * L, 0:RW_W], 0.0)
    g_k = jnp.where(incl, gram[L:2 * L, RW_W:2 * RW_W], 0.0)
    blk16 = (t_i // 16) == (s_i // 16)
    blk32 = (t_i // 32) == (s_i // 32)
    eye = jnp.where(t_i == s_i, 1.0, 0.0)
    n1 = jnp.where(blk16, -a_b, 0.0)
    n2 = _hmm(n1, n1)
    akv = _hmm(a_k, v)
    gkv = _hmm(g_k, v)
    yield
    n4 = _hmm(n2, n2)
    x = eye + n1 + n2 + _hmm(n1, n2)
    yield
    n8 = _hmm(n4, n4)
    x = x + _hmm(x, n4)
    yield
    x = x + _hmm(x, n8)
    yield
    e = _hmm(x, jnp.where(blk32 & jnp.logical_not(blk16), a_b, 0.0))
    yield
    x = x - _hmm(e, x)
    yield
    e = _hmm(x, jnp.where(blk32, 0.0, a_b))
    yield
    tinv = x - _hmm(e, x)
    yield
    p1 = _hmm(tinv, kb)
    p2 = _hmm(tinv, akv)
    yield
    q = rt - _hmm(g_b, p1)
    z = gkv - _hmm(g_b, p2)
    row = lax.broadcasted_iota(jnp.int32, (HEAD_DIM, RW_W), 0)
    col = lax.broadcasted_iota(jnp.int32, (HEAD_DIM, RW_W), 1) % HEAD_DIM
    m_t = jnp.where(row == col, jnp.exp(last), 0.0) - _head_diag_blocks(_dot_tn(bh, p1))
    n_t = _head_diag_blocks(_dot_tn(jnp.concatenate([kh, -bh], axis=0), jnp.concatenate([v, p2], axis=0)))
    return jnp.concatenate([q, m_t], axis=0), z, n_t


def _rwscan_kernel(shf_ref, df_ref, shb_ref, db_ref, yf_ref, yb_ref, hf_scr, hb_scr, *, cps):
    @pl.when(pl.program_id(1) == 0)
    def _():
        hf_scr[...] = jnp.zeros_like(hf_scr)
        hb_scr[...] = jnp.zeros_like(hb_scr)

    fwd_lo = [ci * CHUNK for ci in range(cps)]
    bwd_lo = fwd_lo[::-1]
    local = _in_lockstep(
        [_rw_chunk(shf_ref[lo:lo + CHUNK, :], df_ref[lo:lo + CHUNK, :], False) for lo in fwd_lo]
        + [_rw_chunk(shb_ref[lo:lo + CHUNK, :], db_ref[lo:lo + CHUNK, :], True) for lo in bwd_lo])
    state = [hf_scr[...], hb_scr[...]]
    for ci in range(cps):
        for d, (los, y_ref) in enumerate(((fwd_lo, yf_ref), (bwd_lo, yb_ref))):
            qm, z, n_t = local[d * cps + ci]
            both = _hmm(qm, state[d])
            y_ref[los[ci]:los[ci] + CHUNK, :] = both[0:CHUNK] + z
            state[d] = both[CHUNK:2 * CHUNK] + n_t
    hf_scr[...], hb_scr[...] = state


def _rw_scan(sh, df, db, B, S, cps):
    T = B * S
    rows = cps * CHUNK
    n = S // rows
    fwd = lambda b, c: (b * n + c, 0)
    bwd = lambda b, c: (b * n + n - 1 - c, 0)
    return pl.pallas_call(
        functools.partial(_rwscan_kernel, cps=cps),
        grid=(B, n),
        in_specs=[pl.BlockSpec((rows, 4 * RW_W), fwd), pl.BlockSpec((rows, 3 * RW_W), fwd),
                  pl.BlockSpec((rows, 4 * RW_W), bwd), pl.BlockSpec((rows, 3 * RW_W), bwd)],
        out_specs=[pl.BlockSpec((rows, RW_W), fwd), pl.BlockSpec((rows, RW_W), bwd)],
        out_shape=[jax.ShapeDtypeStruct((T, RW_W), F32)] * 2,
        scratch_shapes=[pltpu.VMEM((HEAD_DIM, RW_W), F32)] * 2,
        compiler_params=_params(("parallel", "arbitrary")),
        name="rw_scan",
    )(sh, df, sh, db)


def _rope(y, cos, sin):
    lane = lax.broadcasted_iota(jnp.int32, y.shape, 1)
    rot = jnp.where(lane % 32 < 16, -pltpu.roll(y, LANE - 16, 1), pltpu.roll(y, 16, 1))
    return y * cos + rot * sin


def _atprep_kernel(q_ref, kv_ref, cos_ref, sin_ref, qn_ref, kn_ref, qo_ref, ko_ref, vo_ref):
    cos, sin = cos_ref[...], sin_ref[...]
    for h in range(AT_HEADS):
        xh = q_ref[:, h * LANE:(h + 1) * LANE]
        ms = jnp.sum(xh * xh, axis=-1, keepdims=True) * (1.0 / HEAD_DIM)
        y = xh * lax.rsqrt(ms + NORM_EPS) * qn_ref[...]
        qo_ref[:, h * LANE:(h + 1) * LANE] = (_rope(y, cos, sin) * Q_SCALE).astype(BF16)
    kx = kv_ref[:, 0:LANE]
    low = lax.broadcasted_iota(jnp.int32, kx.shape, 1) < HEAD_DIM
    sq = kx * kx
    ms = jnp.where(low, jnp.sum(jnp.where(low, sq, 0.0), axis=-1, keepdims=True),
                   jnp.sum(jnp.where(low, 0.0, sq), axis=-1, keepdims=True)) * (1.0 / HEAD_DIM)
    ko_ref[...] = _rope(kx * lax.rsqrt(ms + NORM_EPS) * kn_ref[...], cos, sin).astype(BF16)
    vo_ref[...] = kv_ref[:, LANE:2 * LANE].astype(BF16)


def _at_prep(q, kv, cos, sin, qn, kn, S, tm):
    T = q.shape[0]
    n = S // tm
    tok = lambda w: pl.BlockSpec((tm, w), lambda i: (i, 0))
    pos = pl.BlockSpec((tm, LANE), lambda i: (i % n, 0))
    return pl.pallas_call(
        _atprep_kernel,
        grid=(T // tm,),
        in_specs=[tok(AT_HEADS * LANE), tok(2 * AT_KV_W), pos, pos, _const_spec((1, LANE)), _const_spec((1, LANE))],
        out_specs=[tok(AT_HEADS * LANE), tok(LANE), tok(LANE)],
        out_shape=[jax.ShapeDtypeStruct((T, AT_HEADS * LANE), BF16), jax.ShapeDtypeStruct((T, LANE), BF16),
                   jax.ShapeDtypeStruct((T, LANE), BF16)],
        compiler_params=_params(("parallel",)),
        name="at_prep",
    )(q, kv, cos, sin, qn, kn)


def _attn_kernel(q_ref, k_ref, v_ref, o_ref, s_scr, *, tq, tk, n_k):
    q = jnp.concatenate([q_ref[:, h * LANE:(h + 1) * LANE] for h in range(AT_HEADS)], axis=0)
    m_rows = AT_HEADS * tq

    def scores(c):
        return lax.dot_general(q, k_ref[c * tk:(c + 1) * tk, :], (((1,), (1,)), ((), ())),
                               preferred_element_type=F32)

    def update(c, s, carry):
        m, l, acc = carry
        m_new = jnp.maximum(m, jnp.max(s, axis=-1, keepdims=True))
        p = jnp.exp2(s - m_new)
        alpha = jnp.exp2(m - m_new)
        l = alpha * l + jnp.sum(p, axis=-1, keepdims=True)
        acc = alpha * acc + jnp.dot(p.astype(BF16), v_ref[c * tk:(c + 1) * tk, :], preferred_element_type=F32)
        return m_new, l, acc

    carry = (jnp.full((m_rows, 1), -jnp.inf, F32), jnp.zeros((m_rows, 1), F32), jnp.zeros((m_rows, LANE), F32))
    s_scr[0] = scores(0)
    for c in range(n_k):
        if c + 1 < n_k:
            s_scr[(c + 1) % 2] = scores(c + 1)
        carry = update(c, s_scr[c % 2], carry)
    _, l, acc = carry
    o = acc / l
    for h in range(AT_HEADS):
        o_ref[:, h * LANE:(h + 1) * LANE] = o[h * tq:(h + 1) * tq, :].astype(BF16)


def _attn(qh, kh, vh, B, S, tq, tk):
    T = B * S
    n = S // tq
    return pl.pallas_call(
        functools.partial(_attn_kernel, tq=tq, tk=tk, n_k=S // tk),
        grid=(B, n),
        in_specs=[pl.BlockSpec((tq, AT_HEADS * LANE), lambda b, i: (b * n + i, 0)),
                  pl.BlockSpec((S, LANE), lambda b, i: (b, 0)), pl.BlockSpec((S, LANE), lambda b, i: (b, 0))],
        out_specs=pl.BlockSpec((tq, AT_HEADS * LANE), lambda b, i: (b * n + i, 0)),
        out_shape=jax.ShapeDtypeStruct((T, AT_HEADS * LANE), BF16),
        scratch_shapes=[pltpu.VMEM((2, AT_HEADS * tq, tk), F32)],
        compiler_params=_params(("parallel", "parallel")),
        name="attn",
    )(qh, kh, vh)


def _mlprep_kernel(x_ref, hp_ref, hn_ref, cw_ref, q_ref, k_ref):
    j = pl.program_id(1)
    x = x_ref[...]
    prev_row = jnp.where(j == 0, 0.0, hp_ref[SUBLANE - 1:SUBLANE, :])
    next_row = jnp.where(j == pl.num_programs(1) - 1, 0.0, hn_ref[0:1, :])
    xp, xn = _shift_rows(x, prev_row, next_row)
    y = cw_ref[0:1, :] * xp + cw_ref[1:2, :] * x + cw_ref[2:3, :] * xn
    y = y * _sigmoid(y)
    q_ref[...] = y[:, 0:ML_W]
    k_ref[...] = y[:, ML_W:2 * ML_W] * (HEAD_DIM ** -0.5)


def _ml_prep(ml, cw, B, S, ts):
    T = B * S
    n = S // ts
    main, prev, nxt = _halo_specs(ts, 2 * ML_W, n, T)
    return pl.pallas_call(
        _mlprep_kernel,
        grid=(B, n),
        in_specs=[main, prev, nxt, _const_spec((3, 2 * ML_W))],
        out_specs=[pl.BlockSpec((ts, ML_W), lambda b, j: (b * n + j, 0))] * 2,
        out_shape=[jax.ShapeDtypeStruct((T, ML_W), F32)] * 2,
        compiler_params=_params(("parallel", "parallel")),
        name="ml_prep",
    )(ml, ml, ml, cw)


def _scan_max(x, rev):
    n = x.shape[0]
    row = lax.broadcasted_iota(jnp.int32, x.shape, 0)
    step = 1
    while step < n:
        if rev:
            shifted = jnp.where(row >= n - step, -jnp.inf, pltpu.roll(x, n - step, 0))
        else:
            shifted = jnp.where(row < step, -jnp.inf, pltpu.roll(x, step, 0))
        x = jnp.maximum(x, shifted)
        step *= 2
    return x


def _ml_chunk_local(q, k, v, ig, lf, bd, rev):
    L = CHUNK
    t_i = lax.broadcasted_iota(jnp.int32, (L, ML_W), 0)
    s_i = lax.broadcasted_iota(jnp.int32, (L, ML_W), 1) % L
    tt = lax.broadcasted_iota(jnp.int32, (L, L), 0)
    ss = lax.broadcasted_iota(jnp.int32, (L, L), 1)
    tri = ((ss >= tt) if rev else (ss <= tt)).astype(BF16)
    causal = (s_i >= t_i) if rev else (s_i <= t_i)
    bcol = _sel_dot(tri, lf)
    qk = _dot_nt(q, _head_blockdiag(k))
    yield
    b_last = bcol[0:1, :] if rev else bcol[L - 1:L, :]
    c = ig - bcol
    m_loc = bcol + _scan_max(c, rev)
    c_row = jnp.sum(jnp.where(t_i == s_i, c, 0.0), axis=0, keepdims=True)
    pb = (jnp.exp(jnp.where(causal, bcol + c_row - m_loc, -jnp.inf)) * qk).astype(BF16)
    g = b_last - bcol + ig
    mg = jnp.max(g, axis=0, keepdims=True)
    wgt = jnp.exp(g - mg)
    num = jnp.dot(pb, _head_blockdiag(v).astype(BF16), preferred_element_type=F32)
    den = jnp.dot(pb, bd, preferred_element_type=F32)
    kc = _head_diag_blocks(_dot_tn(k, wgt * v))
    nc = jnp.sum(wgt * k, axis=0, keepdims=True)
    return dict(q=q, bcol=bcol, b_last=b_last, m_loc=m_loc, num=num, den=den, mg=mg, kc=kc, nc=nc)


def _ml_chunk_combine(loc, ct, n, m, bd):
    q = loc["q"]
    inter = loc["bcol"] + m
    m_t = jnp.maximum(loc["m_loc"], inter)
    e_loc, e_int = jnp.exp(loc["m_loc"] - m_t), jnp.exp(inter - m_t)
    num = e_loc * loc["num"] + e_int * _hmm(q, ct)
    den = e_loc * loc["den"] + e_int * jnp.dot((q * n).astype(BF16), bd, preferred_element_type=F32)
    h = num / jnp.maximum(jnp.abs(den), jnp.exp(-m_t))
    m_new = jnp.maximum(loc["b_last"] + m, loc["mg"])
    a1 = jnp.exp(loc["b_last"] + m - m_new)
    a2 = jnp.exp(loc["mg"] - m_new)
    return h, a1 * ct + a2 * loc["kc"], a1 * n + a2 * loc["nc"], m_new


def _mlscan_kernel(qf_ref, kf_ref, vf_ref, gf_ref, qb_ref, kb_ref, vb_ref, gb_ref, bias_ref, ex_ref, bd_ref,
                   hf_ref, hb_ref, ct_scr, n_scr, m_scr, *, cps):
    @pl.when(pl.program_id(1) == 0)
    def _():
        ct_scr[...] = jnp.zeros_like(ct_scr)
        n_scr[...] = jnp.zeros_like(n_scr)
        m_scr[...] = jnp.zeros_like(m_scr)

    bd = bd_ref[...]
    dirs = ((0, False, qf_ref, kf_ref, vf_ref, gf_ref, hf_ref), (1, True, qb_ref, kb_ref, vb_ref, gb_ref, hb_ref))
    gens = []
    for d, rev, q_ref, k_ref, v_ref, g_ref, h_ref in dirs:
        gates = g_ref[...] + bias_ref[...]
        log_sig = jnp.minimum(gates, 0.0) - jnp.log1p(jnp.exp(-jnp.abs(gates)))
        lane = lax.broadcasted_iota(jnp.int32, gates.shape, 1)
        spread = _dot_sel(jnp.where(lane < 2 * ML_HEADS, gates, log_sig), ex_ref[d])
        order = [(cps - 1 - ci) * CHUNK for ci in range(cps)] if rev else [ci * CHUNK for ci in range(cps)]
        gens += [_ml_chunk_local(q_ref[lo:lo + CHUNK, :], k_ref[lo:lo + CHUNK, :], v_ref[lo:lo + CHUNK, :],
                                 spread[lo:lo + CHUNK, 0:ML_W], spread[lo:lo + CHUNK, ML_W:2 * ML_W], bd, rev)
                 for lo in order]
    local = _in_lockstep(gens)
    state = [(ct_scr[d], n_scr[d], m_scr[d]) for d in range(2)]
    for ci in range(cps):
        for d, rev, _, _, _, _, h_ref in dirs:
            lo = (cps - 1 - ci) * CHUNK if rev else ci * CHUNK
            out, *state[d] = _ml_chunk_combine(local[d * cps + ci], *state[d], bd)
            h_ref[lo:lo + CHUNK, :] = out
    for d in range(2):
        ct_scr[d], n_scr[d], m_scr[d] = state[d]


def _ml_scan(mq, mk, ml, gc, bias, ex, bd, B, S, cps):
    T = B * S
    rows = cps * CHUNK
    n = S // rows
    fwd = lambda b, c: (b * n + c, 0)
    bwd = lambda b, c: (b * n + n - 1 - c, 0)
    fwd_v = lambda b, c: (b * n + c, 2)
    bwd_v = lambda b, c: (b * n + n - 1 - c, 2)
    tile = lambda im: pl.BlockSpec((rows, ML_W), im)
    gate = lambda im: pl.BlockSpec((rows, LANE), im)
    return pl.pallas_call(
        functools.partial(_mlscan_kernel, cps=cps),
        grid=(B, n),
        in_specs=[tile(fwd), tile(fwd), tile(fwd_v), gate(fwd), tile(bwd), tile(bwd), tile(bwd_v), gate(bwd),
                  _const_spec((1, LANE)), _const_spec((2, LANE, 2 * ML_W)), _const_spec((ML_W, ML_W))],
        out_specs=[tile(fwd), tile(bwd)],
        out_shape=[jax.ShapeDtypeStruct((T, ML_W), F32)] * 2,
        scratch_shapes=[pltpu.VMEM((2, HEAD_DIM, ML_W), F32), pltpu.VMEM((2, 1, ML_W), F32),
                        pltpu.VMEM((2, 1, ML_W), F32)],
        compiler_params=_params(("parallel", "arbitrary")),
        name="ml_scan",
    )(mq, mk, ml, gc, mq, mk, ml, gc, bias, ex, bd)


def _outproj_kernel(x_ref, yf_ref, yb_ref, sh_ref, df_ref, db_ref, ao_ref, hf_ref, hb_ref, og_ref,
                    rk_ref, lnw_ref, lnb_ref, nw_ref, bd_ref, worw_ref, woat_ref, woml_ref, o_ref):
    bd = bd_ref[...]
    inv = 1.0 / HEAD_DIM
    wkv = yf_ref[...] + yb_ref[...]
    dev = wkv - _dot_sel(wkv, bd) * inv
    var = _dot_sel(dev * dev, bd) * inv
    y = dev * lax.rsqrt(var + RW_LN_EPS) * lnw_ref[...] + lnb_ref[...]
    r, v, g = sh_ref[:, 0:RW_W], sh_ref[:, RW_W:2 * RW_W], sh_ref[:, 3 * RW_W:4 * RW_W]
    bonus = _dot_sel(r * (df_ref[...] + db_ref[...]) * rk_ref[...], bd) * v
    y_rw = (y + bonus) * g
    hm = hf_ref[...] + hb_ref[...]
    hn = hm * lax.rsqrt(_dot_sel(hm * hm, bd) * inv + NORM_EPS) * nw_ref[...]
    y_ml = _sigmoid(og_ref[...]) * hn
    o_ref[...] = (x_ref[...] + _dot(y_rw, worw_ref[...])
                  + jnp.dot(ao_ref[...], woat_ref[...], preferred_element_type=F32) + _dot(y_ml, woml_ref[...]))


def _outproj(xf, yf, yb, sh, df, db, ao, hf, hb, ml, p, tm):
    T = xf.shape[0]
    tok = lambda w, c=0: pl.BlockSpec((tm, w), lambda i: (i, c))
    vec = _const_spec((1, RW_W))
    return pl.pallas_call(
        _outproj_kernel,
        grid=(T // tm,),
        in_specs=[tok(D_MODEL), tok(RW_W), tok(RW_W), tok(4 * RW_W), tok(RW_W, 1), tok(RW_W, 1),
                  tok(AT_HEADS * LANE), tok(ML_W), tok(ML_W), tok(ML_W, 3),
                  vec, vec, vec, vec, _const_spec((RW_W, RW_W)),
                  _const_spec((RW_W, D_MODEL)), _const_spec((AT_HEADS * LANE, D_MODEL)), _const_spec((ML_W, D_MODEL))],
        out_specs=tok(D_MODEL),
        out_shape=jax.ShapeDtypeStruct((T, D_MODEL), F32),
        compiler_params=_params(("parallel",)),
        name="outproj",
    )(xf, yf, yb, sh, df, db, ao, hf, hb, ml, p["rk"], p["lnw"], p["lnb"], p["nw"], p["bd"],
      p["wo_rw"], p["wo_at"], p["wo_ml"])


def _mlp_kernel(h_ref, g_ref, w1_ref, w2_ref, gf_ref, o_ref, *, final):
    h = h_ref[...]
    u = jnp.dot(_rms(h, g_ref[...]).astype(BF16), w1_ref[...], preferred_element_type=F32)
    u = jnp.square(jnp.maximum(u, 0.0)).astype(BF16)
    out = h + jnp.dot(u, w2_ref[...], preferred_element_type=F32)
    if final:
        out = _rms(out, gf_ref[...])
    o_ref[...] = out


def _mlp(h, g, w1, w2, gf, final, tm):
    T = h.shape[0]
    once = pl.Buffered(1)
    return pl.pallas_call(
        functools.partial(_mlp_kernel, final=final),
        grid=(T // tm,),
        in_specs=[pl.BlockSpec((tm, D_MODEL), lambda i: (i, 0)), _const_spec((1, D_MODEL)),
                  pl.BlockSpec((D_MODEL, D_FF), lambda i: (0, 0), pipeline_mode=once),
                  pl.BlockSpec((D_FF, D_MODEL), lambda i: (0, 0), pipeline_mode=once),
                  _const_spec((1, D_MODEL))],
        out_specs=pl.BlockSpec((tm, D_MODEL), lambda i: (i, 0)),
        out_shape=jax.ShapeDtypeStruct((T, D_MODEL), F32),
        compiler_params=_params(("parallel",)),
        name="mlp_final" if final else "mlp",
    )(h, g, w1, w2, gf)


def _rope_tables(seq_len):
    t = np.arange(seq_len)
    n_freq = HEAD_DIM // 4
    inv = ROPE_THETA ** (-jnp.arange(n_freq, dtype=F32) / n_freq)
    ang_r = jnp.asarray(t // GRID_W, F32)[:, None] * inv
    ang_c = jnp.asarray(t % GRID_W, F32)[:, None] * inv
    ang = jnp.concatenate([ang_r, ang_r, ang_c, ang_c] * 2, axis=-1)
    return jnp.cos(ang), jnp.sin(ang)


def _pad_lanes_per_head(w, axis):
    parts = []
    for h in range(AT_HEADS):
        blk = lax.slice_in_dim(w, h * HEAD_DIM, (h + 1) * HEAD_DIM, axis=axis)
        zero = jnp.zeros_like(blk)
        parts += [blk, zero] if h // AT_GROUP == 0 else [zero, blk]
    return jnp.concatenate(parts, axis=axis)


def _layer_params(l, norm1_g, w_in, rw_mu, rw_w0, rw_w2, rw_a0, rw_a2, rw_g2, rw_kk, rw_ka, rw_rk, rw_lnw, rw_lnb,
                  at_qn, at_kn, ml_conv, ml_ib, ml_fb, ml_nw, w_out, norm2_g, mlp_w1, mlp_w2):
    w = w_in[l]
    w_at = w[:, RW_COLS:RW_COLS + AT_COLS]
    w_ml = w[:, RW_COLS + AT_COLS:]
    zero_lora = jnp.zeros((DECAY_LORA, RW_W), F32)
    lora = lambda m: jnp.stack([jnp.concatenate([m[0], zero_lora], 0), jnp.concatenate([zero_lora, m[1]], 0)])
    idx = np.arange(RW_W) // HEAD_DIM
    gate_col = np.arange(LANE)[:, None]
    spread = lambda first: gate_col == first + idx[None, :]
    ex = np.stack([np.concatenate([spread(d * ML_HEADS), spread(2 * ML_HEADS + d * ML_HEADS)], axis=1)
                   for d in range(2)])
    wo = w_out[l]
    return dict(
        g1=norm1_g[l][None, :],
        w_rw=w[:, :RW_COLS].astype(BF16),
        w_q=_pad_lanes_per_head(w_at[:, :AT_W], 1).astype(BF16),
        w_kv=w_at[:, AT_W:].astype(BF16),
        w_ml=w_ml[:, :4 * ML_W].astype(BF16),
        w_g=jnp.pad(w_ml[:, 4 * ML_W:], ((0, 0), (0, LANE - 4 * ML_HEADS))).astype(BF16),
        mu=rw_mu[l][None, :], w0=rw_w0[l], w2=lora(rw_w2[l]), a0=rw_a0[l], a2=lora(rw_a2[l]), g2=rw_g2[l],
        kk=rw_kk[l][None, :], ka=rw_ka[l][None, :],
        bd=jnp.asarray(idx[:, None] == idx[None, :], BF16),
        rk=rw_rk[l].reshape(1, RW_W), lnw=rw_lnw[l][None, :], lnb=rw_lnb[l][None, :],
        qn=jnp.tile(at_qn[l], 2)[None, :], kn=jnp.tile(at_kn[l], 2)[None, :],
        conv=ml_conv[l],
        gbias=jnp.pad(jnp.concatenate([ml_ib[l].reshape(-1), ml_fb[l].reshape(-1)]),
                      (0, LANE - 4 * ML_HEADS))[None, :],
        nw=ml_nw[l][None, :], ex=jnp.asarray(ex, BF16),
        wo_rw=wo[:RW_W].astype(BF16),
        wo_at=_pad_lanes_per_head(wo[RW_W:RW_W + AT_W], 0).astype(BF16),
        wo_ml=wo[RW_W + AT_W:].astype(BF16),
        g2n=norm2_g[l][None, :], w1=mlp_w1[l].astype(BF16), w2m=mlp_w2[l].astype(BF16),
    )


def _tiles(S):
    return dict(tm=512, ts=512, cps=4, tq=128, tk=1024)


def _trunk(x, layers, final_g):
    B, S, _ = x.shape
    t = _tiles(S)
    xf = x.reshape(B * S, D_MODEL)
    cos, sin = _rope_tables(S)
    gf = final_g[None, :]
    for l, p in enumerate(layers):
        rw, q, kv, ml, gc = _inproj(xf, p["g1"], p["w_rw"], p["w_q"], p["w_kv"], p["w_ml"], p["w_g"], t["tm"])
        sh, df, db = _rw_prep(rw, p, B, S, t["ts"])
        yf, yb = _rw_scan(sh, df, db, B, S, t["cps"])
        qh, kh, vh = _at_prep(q, kv, cos, sin, p["qn"], p["kn"], S, t["tm"])
        ao = _attn(qh, kh, vh, B, S, t["tq"], t["tk"])
        mq, mk = _ml_prep(ml, p["conv"], B, S, t["ts"])
        hf, hb = _ml_scan(mq, mk, ml, gc, p["gbias"], p["ex"], p["bd"], B, S, t["cps"])
        h = _outproj(xf, yf, yb, sh, df, db, ao, hf, hb, ml, p, t["tm"])
        xf = _mlp(h, p["g2n"], p["w1"], p["w2m"], gf, l == len(layers) - 1, t["tm"])
    return xf.reshape(B, S, D_MODEL)


def kernel(x_prompt, x_sample, norm1_g, w_in, rw_mu, rw_w0, rw_w2, rw_a0, rw_a2, rw_g2, rw_kk, rw_ka, rw_rk,
           rw_lnw, rw_lnb, at_qn, at_kn, ml_conv, ml_ib, ml_fb, ml_nw, w_out, norm2_g, mlp_w1, mlp_w2, final_g):
    layers = [_layer_params(l, norm1_g, w_in, rw_mu, rw_w0, rw_w2, rw_a0, rw_a2, rw_g2, rw_kk, rw_ka, rw_rk,
                            rw_lnw, rw_lnb, at_qn, at_kn, ml_conv, ml_ib, ml_fb, ml_nw, w_out, norm2_g,
                            mlp_w1, mlp_w2) for l in range(DEPTH)]
    return _trunk(x_prompt, layers, final_g), _trunk(x_sample, layers, final_g)
```

```python
import functools

import numpy as np
import jax
import jax.numpy as jnp
from jax import lax
from jax.experimental import pallas as pl
from jax.experimental.pallas import tpu as pltpu

F32 = jnp.float32
BF16 = jnp.bfloat16

D_MODEL = 1024
DEPTH = 2
GRID_W = 64
HEAD_DIM = 64
RW_W = 256
AT_HEADS = 8
AT_KV_HEADS = 2
AT_GROUP = 4
AT_W = 512
AT_KV_W = 128
ML_HEADS = 4
ML_W = 256
DECAY_LORA = 64
AAA_LORA = 64
GATE_LORA = 128
D_FF = 4096
ROPE_THETA = 10000.0
NORM_EPS = 1e-6
RW_LN_EPS = 64e-5
RW_COLS = 1152
AT_COLS = 768
ML_COLS = 1040
CHUNK = 64
LANE = 128
SUBLANE = 8
VMEM_LIMIT = 56 * 1024 * 1024
Q_SCALE = HEAD_DIM ** -0.5 * float(np.log2(np.e))


def _params(sem):
    return pltpu.CompilerParams(dimension_semantics=sem, vmem_limit_bytes=VMEM_LIMIT)


def _dot(a, b):
    return jnp.dot(a.astype(BF16), b.astype(BF16), preferred_element_type=F32)


def _dot_nt(a, b):
    return lax.dot_general(a.astype(BF16), b.astype(BF16), (((1,), (1,)), ((), ())), preferred_element_type=F32)


def _dot_tn(a, b):
    return lax.dot_general(a.astype(BF16), b.astype(BF16), (((0,), (0,)), ((), ())), preferred_element_type=F32)


def _split3(x):
    hi = x.astype(BF16)
    r = x - hi.astype(F32)
    mid = r.astype(BF16)
    lo = (r - mid.astype(F32)).astype(BF16)
    return hi, mid, lo


def _dot_sel(x, sel):
    hi, mid, lo = _split3(x)
    d = lambda a: jnp.dot(a, sel, preferred_element_type=F32)
    return d(hi) + d(mid) + d(lo)


def _sel_dot(sel, x):
    hi, mid, lo = _split3(x)
    d = lambda a: jnp.dot(sel, a, preferred_element_type=F32)
    return d(hi) + d(mid) + d(lo)


def _rms(x, g):
    return x * lax.rsqrt(jnp.mean(x * x, axis=-1, keepdims=True) + NORM_EPS) * g


def _softplus(z):
    return jnp.maximum(z, 0.0) + jnp.log1p(jnp.exp(-jnp.abs(z)))


def _sigmoid(z):
    return 1.0 / (1.0 + jnp.exp(-z))


def _const_spec(shape):
    nd = len(shape)
    return pl.BlockSpec(shape, lambda *_: (0,) * nd)


def _shift_rows(x, prev_row, next_row):
    n = x.shape[0]
    row = lax.broadcasted_iota(jnp.int32, (n, 1), 0)
    xp = jnp.where(row == 0, prev_row, pltpu.roll(x, 1, 0))
    xn = jnp.where(row == n - 1, next_row, pltpu.roll(x, n - 1, 0))
    return xp, xn


def _halo_specs(ts, width, n_seq_tiles, n_rows):
    per = ts // SUBLANE
    last = n_rows // SUBLANE - 1
    main = pl.BlockSpec((ts, width), lambda b, j: (b * n_seq_tiles + j, 0))
    prev = pl.BlockSpec((SUBLANE, width), lambda b, j: (jnp.maximum((b * n_seq_tiles + j) * per - 1, 0), 0))
    nxt = pl.BlockSpec((SUBLANE, width), lambda b, j: (jnp.minimum((b * n_seq_tiles + j + 1) * per, last), 0))
    return main, prev, nxt


def _rope(y, cos, sin):
    lane = lax.broadcasted_iota(jnp.int32, y.shape, 1)
    rot = jnp.where(lane % 32 < 16, -pltpu.roll(y, LANE - 16, 1), pltpu.roll(y, 16, 1))
    return y * cos + rot * sin


def _half_lane_ms(x, low):
    sq = x * x
    return jnp.where(low, jnp.sum(jnp.where(low, sq, 0.0), axis=-1, keepdims=True),
                     jnp.sum(jnp.where(low, 0.0, sq), axis=-1, keepdims=True)) * (1.0 / HEAD_DIM)


def _inproj_kernel(x_ref, g_ref, wrw_ref, wat_ref, wml_ref, wg_ref, cos_ref, sin_ref, qn_ref, kn_ref,
                   rw_ref, ml_ref, gc_ref, qo_ref, ko_ref, va_ref, vb_ref):
    xb = _rms(x_ref[...], g_ref[...]).astype(BF16)
    for w_ref, o_ref in ((wrw_ref, rw_ref), (wml_ref, ml_ref), (wg_ref, gc_ref)):
        o_ref[...] = jnp.dot(xb, w_ref[...], preferred_element_type=F32)
    at = jnp.dot(xb, wat_ref[...], preferred_element_type=F32)
    cos, sin = cos_ref[...], sin_ref[...]
    low = lax.broadcasted_iota(jnp.int32, (at.shape[0], LANE), 1) < HEAD_DIM
    for pair in range(AT_HEADS // 2):
        x = at[:, pair * LANE:(pair + 1) * LANE]
        y = _rope(x * lax.rsqrt(_half_lane_ms(x, low) + NORM_EPS) * qn_ref[...], cos, sin) * Q_SCALE
        swapped = pltpu.roll(y, HEAD_DIM, 1)
        if 2 * pair < AT_GROUP:
            even, odd = jnp.where(low, y, 0.0), jnp.where(low, swapped, 0.0)
        else:
            even, odd = jnp.where(low, 0.0, swapped), jnp.where(low, 0.0, y)
        qo_ref[:, 2 * pair * LANE:(2 * pair + 1) * LANE] = even.astype(BF16)
        qo_ref[:, (2 * pair + 1) * LANE:(2 * pair + 2) * LANE] = odd.astype(BF16)
    kx = at[:, AT_W:AT_W + LANE]
    ko_ref[...] = _rope(kx * lax.rsqrt(_half_lane_ms(kx, low) + NORM_EPS) * kn_ref[...], cos, sin).astype(BF16)
    v = at[:, AT_W + LANE:AT_W + 2 * LANE]
    va_ref[...] = jnp.where(low, v, 1.0).astype(BF16)
    vb_ref[...] = jnp.where(low, 1.0, v).astype(BF16)


def _inproj(xf, p, cos, sin, S, tm):
    T = xf.shape[0]
    n = S // tm
    f32_w = (RW_COLS, 4 * ML_W, LANE)
    bf16_w = (AT_HEADS * LANE, LANE, LANE, LANE)
    tok = lambda w: pl.BlockSpec((tm, w), lambda i: (i, 0))
    pos = pl.BlockSpec((tm, LANE), lambda i: (i % n, 0))
    return pl.pallas_call(
        _inproj_kernel,
        grid=(T // tm,),
        in_specs=[tok(D_MODEL), _const_spec((1, D_MODEL)), _const_spec((D_MODEL, RW_COLS)),
                  _const_spec((D_MODEL, AT_COLS)), _const_spec((D_MODEL, 4 * ML_W)), _const_spec((D_MODEL, LANE)),
                  pos, pos, _const_spec((1, LANE)), _const_spec((1, LANE))],
        out_specs=[tok(w) for w in f32_w + bf16_w],
        out_shape=[jax.ShapeDtypeStruct((T, w), F32) for w in f32_w]
        + [jax.ShapeDtypeStruct((T, w), BF16) for w in bf16_w],
        compiler_params=_params(("parallel",)),
        name="inproj",
    )(xf, p["g1"], p["w_rw"], p["w_at"], p["w_ml"], p["w_g"], cos, sin, p["qn"], p["kn"])


def _rwprep_kernel(x_ref, hp_ref, hn_ref, mu_ref, w0_ref, w2_ref, a0_ref, a2_ref, g2_ref, kk_ref, ka_ref, bd_ref,
                   sh_ref, df_ref, db_ref, gate_ref):
    j = pl.program_id(1)
    x = x_ref[...]
    prev_row = jnp.where(j == 0, 0.0, hp_ref[SUBLANE - 1:SUBLANE, :])
    next_row = jnp.where(j == pl.num_programs(1) - 1, 0.0, hn_ref[0:1, :])
    xp, xn = _shift_rows(x, prev_row, next_row)
    xs = x + (0.5 * (xp + xn) - x) * mu_ref[...]
    r, k, v = xs[:, 0:RW_W], xs[:, RW_W:2 * RW_W], xs[:, 2 * RW_W:3 * RW_W]
    wd = xs[:, 3 * RW_W:3 * RW_W + LANE]
    ad = xs[:, 3 * RW_W + LANE:3 * RW_W + 2 * LANE]
    gd = xs[:, 3 * RW_W + 2 * LANE:3 * RW_W + 3 * LANE]
    kk = k * kk_ref[...]
    ss = _dot_sel(kk * kk, bd_ref[...])
    kap = kk / jnp.maximum(jnp.sqrt(ss), 1e-12)
    sh_ref[:, 0:RW_W] = r
    sh_ref[:, RW_W:2 * RW_W] = v
    sh_ref[:, 2 * RW_W:3 * RW_W] = kap
    gate_ref[...] = _dot(_sigmoid(gd), g2_ref[...])
    tw = jnp.tanh(wd)
    for d, o_ref in ((0, df_ref), (1, db_ref)):
        wl = _dot(tw, w2_ref[d])
        o_ref[:, 0:RW_W] = -jnp.exp(-_softplus(-(w0_ref[d:d + 1, :] + wl)) - 0.5)
        a = _sigmoid(a0_ref[d:d + 1, :] + _dot(ad, a2_ref[d]))
        o_ref[:, RW_W:2 * RW_W] = k * (1.0 + (a - 1.0) * ka_ref[...])
        o_ref[:, 2 * RW_W:3 * RW_W] = a


def _rw_prep(rw, p, B, S, ts):
    T = B * S
    n = S // ts
    main, prev, nxt = _halo_specs(ts, RW_COLS, n, T)
    out_w = (3 * RW_W, 3 * RW_W, 3 * RW_W, RW_W)
    return pl.pallas_call(
        _rwprep_kernel,
        grid=(B, n),
        in_specs=[main, prev, nxt, _const_spec((1, RW_COLS)), _const_spec((2, RW_W)),
                  _const_spec((2, LANE, RW_W)), _const_spec((2, RW_W)), _const_spec((2, LANE, RW_W)),
                  _const_spec((GATE_LORA, RW_W)), _const_spec((1, RW_W)), _const_spec((1, RW_W)),
                  _const_spec((RW_W, RW_W))],
        out_specs=[pl.BlockSpec((ts, w), lambda b, j: (b * n + j, 0)) for w in out_w],
        out_shape=[jax.ShapeDtypeStruct((T, w), F32) for w in out_w],
        compiler_params=_params(("parallel", "parallel")),
        name="rw_prep",
    )(rw, rw, rw, p["mu"], p["w0"], p["w2"], p["a0"], p["a2"], p["g2"], p["kk"], p["ka"], p["bd"])


def _head_blockdiag(y):
    lane = lax.broadcasted_iota(jnp.int32, y.shape, 1) // HEAD_DIM
    return jnp.concatenate([jnp.where(lane == h, y, 0.0) for h in range(4)], axis=0)


def _hmm(x, y):
    return _dot(x, _head_blockdiag(y))


def _head_diag_blocks(f):
    lane = lax.broadcasted_iota(jnp.int32, (HEAD_DIM, 4 * HEAD_DIM), 1) // HEAD_DIM
    out = jnp.where(lane == 0, f[0:HEAD_DIM, :], 0.0)
    for h in range(1, 4):
        out = out + jnp.where(lane == h, f[h * HEAD_DIM:(h + 1) * HEAD_DIM, :], 0.0)
    return out


def _in_lockstep(gens):
    out = [None] * len(gens)
    live = list(enumerate(gens))
    while live:
        still = []
        for i, g in live:
            try:
                next(g)
                still.append((i, g))
            except StopIteration as done:
                out[i] = done.value
        live = still
    return out


def _rw_chunk(sh, dd, rev):
    L = CHUNK
    r, v, kap = sh[:, 0:RW_W], sh[:, RW_W:2 * RW_W], sh[:, 2 * RW_W:3 * RW_W]
    lw, kd, a = dd[:, 0:RW_W], dd[:, RW_W:2 * RW_W], dd[:, 2 * RW_W:3 * RW_W]
    t_i = lax.broadcasted_iota(jnp.int32, (L, RW_W), 0)
    s_i = lax.broadcasted_iota(jnp.int32, (L, RW_W), 1) % L
    tt = lax.broadcasted_iota(jnp.int32, (L, L), 0)
    ss = lax.broadcasted_iota(jnp.int32, (L, L), 1)
    if rev:
        tri = (ss >= tt).astype(BF16)
        strict, incl = s_i > t_i, s_i >= t_i
    else:
        tri = (ss <= tt).astype(BF16)
        strict, incl = s_i < t_i, s_i <= t_i
    cs = _sel_dot(tri, lw)
    yield
    last = cs[0:1, :] if rev else cs[L - 1:L, :]
    w_in, w_inv, w_ex, w_rem = jnp.exp(cs), jnp.exp(-cs), jnp.exp(cs - lw), jnp.exp(last - cs)
    b = kap * a
    rt, kt, bt, kb = r * w_in, kd * w_inv, b * w_inv, kap * w_ex
    bh, kh = b * w_rem, kd * w_rem
    gram = _dot_nt(jnp.concatenate([kb, rt], axis=0),
                   jnp.concatenate([_head_blockdiag(bt), _head_blockdiag(kt)], axis=0))
    yield
    a_b = jnp.where(strict, gram[0:L, 0:RW_W], 0.0)
    a_k = jnp.where(strict, gram[0:L, RW_W:2 * RW_W], 0.0)
    g_b = jnp.where(incl, gram[L:2 * L, 0:RW_W], 0.0)
    g_k = jnp.where(incl, gram[L:2 * L, RW_W:2 * RW_W], 0.0)
    blk16 = (t_i // 16) == (s_i // 16)
    blk32 = (t_i // 32) == (s_i // 32)
    eye = jnp.where(t_i == s_i, 1.0, 0.0)
    n1 = jnp.where(blk16, -a_b, 0.0)
    n2 = _hmm(n1, n1)
    akv = _hmm(a_k, v)
    gkv = _hmm(g_k, v)
    yield
    n4 = _hmm(n2, n2)
    x = eye + n1 + n2 + _hmm(n1, n2)
    yield
    n8 = _hmm(n4, n4)
    x = x + _hmm(x, n4)
    yield
    x = x + _hmm(x, n8)
    yield
    e = _hmm(x, jnp.where(blk32 & jnp.logical_not(blk16), a_b, 0.0))
    yield
    x = x - _hmm(e, x)
    yield
    e = _hmm(x, jnp.where(blk32, 0.0, a_b))
    yield
    tinv = x - _hmm(e, x)
    yield
    p1 = _hmm(tinv, kb)
    p2 = _hmm(tinv, akv)
    yield
    q = rt - _hmm(g_b, p1)
    z = gkv - _hmm(g_b, p2)
    row = lax.broadcasted_iota(jnp.int32, (HEAD_DIM, RW_W), 0)
    col = lax.broadcasted_iota(jnp.int32, (HEAD_DIM, RW_W), 1) % HEAD_DIM
    m_t = jnp.where(row == col, jnp.exp(last), 0.0) - _head_diag_blocks(_dot_tn(bh, p1))
    n_t = _head_diag_blocks(_dot_tn(jnp.concatenate([kh, -bh], axis=0), jnp.concatenate([v, p2], axis=0)))
    return jnp.concatenate([q, m_t], axis=0), z, n_t


def _rwscan_kernel(shf_ref, df_ref, shb_ref, db_ref, yf_ref, yb_ref, hf_scr, hb_scr, *, cps):
    @pl.when(pl.program_id(1) == 0)
    def _():
        hf_scr[...] = jnp.zeros_like(hf_scr)
        hb_scr[...] = jnp.zeros_like(hb_scr)

    fwd_lo = [ci * CHUNK for ci in range(cps)]
    bwd_lo = fwd_lo[::-1]
    local = _in_lockstep(
        [_rw_chunk(shf_ref[lo:lo + CHUNK, :], df_ref[lo:lo + CHUNK, :], False) for lo in fwd_lo]
        + [_rw_chunk(shb_ref[lo:lo + CHUNK, :], db_ref[lo:lo + CHUNK, :], True) for lo in bwd_lo])
    state = [hf_scr[...], hb_scr[...]]
    for ci in range(cps):
        for d, (los, y_ref) in enumerate(((fwd_lo, yf_ref), (bwd_lo, yb_ref))):
            qm, z, n_t = local[d * cps + ci]
            both = _hmm(qm, state[d])
            y_ref[los[ci]:los[ci] + CHUNK, :] = both[0:CHUNK] + z
            state[d] = both[CHUNK:2 * CHUNK] + n_t
    hf_scr[...], hb_scr[...] = state


def _rw_scan(sh, df, db, B, S, cps):
    T = B * S
    rows = cps * CHUNK
    n = S // rows
    fwd = lambda b, c: (b * n + c, 0)
    bwd = lambda b, c: (b * n + n - 1 - c, 0)
    return pl.pallas_call(
        functools.partial(_rwscan_kernel, cps=cps),
        grid=(B, n),
        in_specs=[pl.BlockSpec((rows, 3 * RW_W), fwd), pl.BlockSpec((rows, 3 * RW_W), fwd),
                  pl.BlockSpec((rows, 3 * RW_W), bwd), pl.BlockSpec((rows, 3 * RW_W), bwd)],
        out_specs=[pl.BlockSpec((rows, RW_W), fwd), pl.BlockSpec((rows, RW_W), bwd)],
        out_shape=[jax.ShapeDtypeStruct((T, RW_W), F32)] * 2,
        scratch_shapes=[pltpu.VMEM((HEAD_DIM, RW_W), F32)] * 2,
        compiler_params=_params(("parallel", "arbitrary")),
        name="rw_scan",
    )(sh, df, sh, db)


def _attn_kernel(q_ref, k_ref, va_ref, vb_ref, o_ref, s_scr, *, tq, tk, n_k):
    q = jnp.concatenate([q_ref[:, h * LANE:(h + 1) * LANE] for h in range(AT_HEADS)], axis=0)
    m_rows = AT_HEADS * tq
    half = m_rows // 2

    def scores(c):
        return lax.dot_general(q, k_ref[c * tk:(c + 1) * tk, :], (((1,), (1,)), ((), ())),
                               preferred_element_type=F32)

    def update(c, s, carry):
        m, acc = carry
        m_new = jnp.maximum(m, jnp.max(s, axis=-1, keepdims=True))
        p = jnp.exp2((s - m_new).astype(BF16))
        keys = slice(c * tk, (c + 1) * tk)
        pv = jnp.concatenate([jnp.dot(p[0:half], va_ref[keys, :], preferred_element_type=F32),
                              jnp.dot(p[half:m_rows], vb_ref[keys, :], preferred_element_type=F32)], axis=0)
        return m_new, jnp.exp2(m - m_new) * acc + pv

    carry = (jnp.full((m_rows, 1), -jnp.inf, F32), jnp.zeros((m_rows, LANE), F32))
    s_scr[0] = scores(0)
    for c in range(n_k):
        if c + 1 < n_k:
            s_scr[(c + 1) % 2] = scores(c + 1)
        carry = update(c, s_scr[c % 2], carry)
    acc = carry[1]
    o = acc / pltpu.roll(acc, HEAD_DIM, 1)
    low = lax.broadcasted_iota(jnp.int32, (tq, LANE), 1) < HEAD_DIM
    for pair in range(AT_HEADS // 2):
        even, odd = o[2 * pair * tq:(2 * pair + 1) * tq, :], o[(2 * pair + 1) * tq:(2 * pair + 2) * tq, :]
        if 2 * pair < AT_GROUP:
            both = jnp.where(low, even, pltpu.roll(odd, HEAD_DIM, 1))
        else:
            both = jnp.where(low, pltpu.roll(even, HEAD_DIM, 1), odd)
        o_ref[:, pair * LANE:(pair + 1) * LANE] = both.astype(BF16)


def _attn(qh, kh, va, vb, B, S, tq, tk):
    T = B * S
    n = S // tq
    seq = pl.BlockSpec((S, LANE), lambda b, i: (b, 0))
    return pl.pallas_call(
        functools.partial(_attn_kernel, tq=tq, tk=tk, n_k=S // tk),
        grid=(B, n),
        in_specs=[pl.BlockSpec((tq, AT_HEADS * LANE), lambda b, i: (b * n + i, 0)),
                  seq, seq, seq],
        out_specs=pl.BlockSpec((tq, AT_W), lambda b, i: (b * n + i, 0)),
        out_shape=jax.ShapeDtypeStruct((T, AT_W), BF16),
        scratch_shapes=[pltpu.VMEM((2, AT_HEADS * tq, tk), F32)],
        compiler_params=_params(("parallel", "parallel")),
        name="attn",
    )(qh, kh, va, vb)


def _mlprep_kernel(x_ref, hp_ref, hn_ref, cw_ref, q_ref, k_ref):
    j = pl.program_id(1)
    x = x_ref[...]
    prev_row = jnp.where(j == 0, 0.0, hp_ref[SUBLANE - 1:SUBLANE, :])
    next_row = jnp.where(j == pl.num_programs(1) - 1, 0.0, hn_ref[0:1, :])
    xp, xn = _shift_rows(x, prev_row, next_row)
    y = cw_ref[0:1, :] * xp + cw_ref[1:2, :] * x + cw_ref[2:3, :] * xn
    y = y * _sigmoid(y)
    q_ref[...] = y[:, 0:ML_W]
    k_ref[...] = y[:, ML_W:2 * ML_W] * (HEAD_DIM ** -0.5)


def _ml_prep(ml, cw, B, S, ts):
    T = B * S
    n = S // ts
    main, prev, nxt = _halo_specs(ts, 2 * ML_W, n, T)
    return pl.pallas_call(
        _mlprep_kernel,
        grid=(B, n),
        in_specs=[main, prev, nxt, _const_spec((3, 2 * ML_W))],
        out_specs=[pl.BlockSpec((ts, ML_W), lambda b, j: (b * n + j, 0))] * 2,
        out_shape=[jax.ShapeDtypeStruct((T, ML_W), F32)] * 2,
        compiler_params=_params(("parallel", "parallel")),
        name="ml_prep",
    )(ml, ml, ml, cw)


def _scan_max(x, rev):
    n = x.shape[0]
    row = lax.broadcasted_iota(jnp.int32, x.shape, 0)
    step = 1
    while step < n:
        if rev:
            shifted = jnp.where(row >= n - step, -jnp.inf, pltpu.roll(x, n - step, 0))
        else:
            shifted = jnp.where(row < step, -jnp.inf, pltpu.roll(x, step, 0))
        x = jnp.maximum(x, shifted)
        step *= 2
    return x


def _ml_chunk_local(q, k, v, ig, lf, bd, rev):
    L = CHUNK
    t_i = lax.broadcasted_iota(jnp.int32, (L, ML_W), 0)
    s_i = lax.broadcasted_iota(jnp.int32, (L, ML_W), 1) % L
    tt = lax.broadcasted_iota(jnp.int32, (L, L), 0)
    ss = lax.broadcasted_iota(jnp.int32, (L, L), 1)
    tri = ((ss >= tt) if rev else (ss <= tt)).astype(BF16)
    causal = (s_i >= t_i) if rev else (s_i <= t_i)
    bcol = _sel_dot(tri, lf)
    qk = _dot_nt(q, _head_blockdiag(k))
    yield
    b_last = bcol[0:1, :] if rev else bcol[L - 1:L, :]
    c = ig - bcol
    m_loc = bcol + _scan_max(c, rev)
    c_row = jnp.sum(jnp.where(t_i == s_i, c, 0.0), axis=0, keepdims=True)
    pb = (jnp.exp(jnp.where(causal, bcol + c_row - m_loc, -jnp.inf)) * qk).astype(BF16)
    g = b_last - bcol + ig
    mg = jnp.max(g, axis=0, keepdims=True)
    wgt = jnp.exp(g - mg)
    num = jnp.dot(pb, _head_blockdiag(v).astype(BF16), preferred_element_type=F32)
    den = jnp.dot(pb, bd, preferred_element_type=F32)
    kc = _head_diag_blocks(_dot_tn(k, wgt * v))
    nc = jnp.sum(wgt * k, axis=0, keepdims=True)
    return dict(q=q, bcol=bcol, b_last=b_last, m_loc=m_loc, num=num, den=den, mg=mg, kc=kc, nc=nc)


def _ml_chunk_combine(loc, ct, n, m, bd):
    q = loc["q"]
    inter = loc["bcol"] + m
    m_t = jnp.maximum(loc["m_loc"], inter)
    e_loc, e_int = jnp.exp(loc["m_loc"] - m_t), jnp.exp(inter - m_t)
    num = e_loc * loc["num"] + e_int * _hmm(q, ct)
    den = e_loc * loc["den"] + e_int * jnp.dot((q * n).astype(BF16), bd, preferred_element_type=F32)
    h = num / jnp.maximum(jnp.abs(den), jnp.exp(-m_t))
    m_new = jnp.maximum(loc["b_last"] + m, loc["mg"])
    a1 = jnp.exp(loc["b_last"] + m - m_new)
    a2 = jnp.exp(loc["mg"] - m_new)
    return h, a1 * ct + a2 * loc["kc"], a1 * n + a2 * loc["nc"], m_new


def _mlscan_kernel(qf_ref, kf_ref, vf_ref, gf_ref, qb_ref, kb_ref, vb_ref, gb_ref, bias_ref, ex_ref, bd_ref,
                   hf_ref, hb_ref, ct_scr, n_scr, m_scr, *, cps):
    @pl.when(pl.program_id(1) == 0)
    def _():
        ct_scr[...] = jnp.zeros_like(ct_scr)
        n_scr[...] = jnp.zeros_like(n_scr)
        m_scr[...] = jnp.zeros_like(m_scr)

    bd = bd_ref[...]
    dirs = ((0, False, qf_ref, kf_ref, vf_ref, gf_ref, hf_ref), (1, True, qb_ref, kb_ref, vb_ref, gb_ref, hb_ref))
    gens = []
    for d, rev, q_ref, k_ref, v_ref, g_ref, h_ref in dirs:
        gates = g_ref[...] + bias_ref[...]
        log_sig = jnp.minimum(gates, 0.0) - jnp.log1p(jnp.exp(-jnp.abs(gates)))
        lane = lax.broadcasted_iota(jnp.int32, gates.shape, 1)
        spread = _dot_sel(jnp.where(lane < 2 * ML_HEADS, gates, log_sig), ex_ref[d])
        order = [(cps - 1 - ci) * CHUNK for ci in range(cps)] if rev else [ci * CHUNK for ci in range(cps)]
        gens += [_ml_chunk_local(q_ref[lo:lo + CHUNK, :], k_ref[lo:lo + CHUNK, :], v_ref[lo:lo + CHUNK, :],
                                 spread[lo:lo + CHUNK, 0:ML_W], spread[lo:lo + CHUNK, ML_W:2 * ML_W], bd, rev)
                 for lo in order]
    local = _in_lockstep(gens)
    state = [(ct_scr[d], n_scr[d], m_scr[d]) for d in range(2)]
    for ci in range(cps):
        for d, rev, _, _, _, _, h_ref in dirs:
            lo = (cps - 1 - ci) * CHUNK if rev else ci * CHUNK
            out, *state[d] = _ml_chunk_combine(local[d * cps + ci], *state[d], bd)
            h_ref[lo:lo + CHUNK, :] = out
    for d in range(2):
        ct_scr[d], n_scr[d], m_scr[d] = state[d]


def _ml_scan(mq, mk, ml, gc, bias, ex, bd, B, S, cps):
    T = B * S
    rows = cps * CHUNK
    n = S // rows
    fwd = lambda b, c: (b * n + c, 0)
    bwd = lambda b, c: (b * n + n - 1 - c, 0)
    fwd_v = lambda b, c: (b * n + c, 2)
    bwd_v = lambda b, c: (b * n + n - 1 - c, 2)
    tile = lambda im: pl.BlockSpec((rows, ML_W), im)
    gate = lambda im: pl.BlockSpec((rows, LANE), im)
    return pl.pallas_call(
        functools.partial(_mlscan_kernel, cps=cps),
        grid=(B, n),
        in_specs=[tile(fwd), tile(fwd), tile(fwd_v), gate(fwd), tile(bwd), tile(bwd), tile(bwd_v), gate(bwd),
                  _const_spec((1, LANE)), _const_spec((2, LANE, 2 * ML_W)), _const_spec((ML_W, ML_W))],
        out_specs=[tile(fwd), tile(bwd)],
        out_shape=[jax.ShapeDtypeStruct((T, ML_W), F32)] * 2,
        scratch_shapes=[pltpu.VMEM((2, HEAD_DIM, ML_W), F32), pltpu.VMEM((2, 1, ML_W), F32),
                        pltpu.VMEM((2, 1, ML_W), F32)],
        compiler_params=_params(("parallel", "arbitrary")),
        name="ml_scan",
    )(mq, mk, ml, gc, mq, mk, ml, gc, bias, ex, bd)


def _outproj_kernel(x_ref, yf_ref, yb_ref, r_ref, v_ref, g_ref, df_ref, db_ref, ao_ref, hf_ref, hb_ref, og_ref,
                    rk_ref, lnw_ref, lnb_ref, nw_ref, bd_ref, worw_ref, woat_ref, woml_ref, o_ref):
    bd = bd_ref[...]
    inv = 1.0 / HEAD_DIM
    wkv = yf_ref[...] + yb_ref[...]
    dev = wkv - _dot_sel(wkv, bd) * inv
    var = _dot_sel(dev * dev, bd) * inv
    y = dev * lax.rsqrt(var + RW_LN_EPS) * lnw_ref[...] + lnb_ref[...]
    bonus = _dot_sel(r_ref[...] * (df_ref[...] + db_ref[...]) * rk_ref[...], bd) * v_ref[...]
    y_rw = (y + bonus) * g_ref[...]
    hm = hf_ref[...] + hb_ref[...]
    hn = hm * lax.rsqrt(_dot_sel(hm * hm, bd) * inv + NORM_EPS) * nw_ref[...]
    y_ml = _sigmoid(og_ref[...]) * hn
    o_ref[...] = (x_ref[...] + _dot(y_rw, worw_ref[...])
                  + jnp.dot(ao_ref[...], woat_ref[...], preferred_element_type=F32) + _dot(y_ml, woml_ref[...]))


def _outproj(xf, yf, yb, sh, gate, df, db, ao, hf, hb, ml, p, tm):
    T = xf.shape[0]
    tok = lambda w, c=0: pl.BlockSpec((tm, w), lambda i: (i, c))
    vec = _const_spec((1, RW_W))
    return pl.pallas_call(
        _outproj_kernel,
        grid=(T // tm,),
        in_specs=[tok(D_MODEL), tok(RW_W), tok(RW_W), tok(RW_W, 0), tok(RW_W, 1), tok(RW_W), tok(RW_W, 1), tok(RW_W, 1),
                  tok(AT_W), tok(ML_W), tok(ML_W), tok(ML_W, 3),
                  vec, vec, vec, vec, _const_spec((RW_W, RW_W)),
                  _const_spec((RW_W, D_MODEL)), _const_spec((AT_W, D_MODEL)), _const_spec((ML_W, D_MODEL))],
        out_specs=tok(D_MODEL),
        out_shape=jax.ShapeDtypeStruct((T, D_MODEL), F32),
        compiler_params=_params(("parallel",)),
        name="outproj",
    )(xf, yf, yb, sh, sh, gate, df, db, ao, hf, hb, ml, p["rk"], p["lnw"], p["lnb"], p["nw"], p["bd"],
      p["wo_rw"], p["wo_at"], p["wo_ml"])


def _mlp_kernel(h_ref, g_ref, w1_ref, w2_ref, gf_ref, o_ref, *, final):
    h = h_ref[...]
    u = jnp.dot(_rms(h, g_ref[...]).astype(BF16), w1_ref[...], preferred_element_type=F32)
    u = jnp.square(jnp.maximum(u, 0.0)).astype(BF16)
    out = h + jnp.dot(u, w2_ref[...], preferred_element_type=F32)
    if final:
        out = _rms(out, gf_ref[...])
    o_ref[...] = out


def _mlp(h, g, w1, w2, gf, final, tm):
    T = h.shape[0]
    once = pl.Buffered(1)
    return pl.pallas_call(
        functools.partial(_mlp_kernel, final=final),
        grid=(T // tm,),
        in_specs=[pl.BlockSpec((tm, D_MODEL), lambda i: (i, 0)), _const_spec((1, D_MODEL)),
                  pl.BlockSpec((D_MODEL, D_FF), lambda i: (0, 0), pipeline_mode=once),
                  pl.BlockSpec((D_FF, D_MODEL), lambda i: (0, 0), pipeline_mode=once),
                  _const_spec((1, D_MODEL))],
        out_specs=pl.BlockSpec((tm, D_MODEL), lambda i: (i, 0)),
        out_shape=jax.ShapeDtypeStruct((T, D_MODEL), F32),
        compiler_params=_params(("parallel",)),
        name="mlp_final" if final else "mlp",
    )(h, g, w1, w2, gf)


def _rope_tables(seq_len):
    t = np.arange(seq_len)
    n_freq = HEAD_DIM // 4
    inv = ROPE_THETA ** (-jnp.arange(n_freq, dtype=F32) / n_freq)
    ang_r = jnp.asarray(t // GRID_W, F32)[:, None] * inv
    ang_c = jnp.asarray(t % GRID_W, F32)[:, None] * inv
    ang = jnp.concatenate([ang_r, ang_r, ang_c, ang_c] * 2, axis=-1)
    return jnp.cos(ang), jnp.sin(ang)


def _layer_params(l, norm1_g, w_in, rw_mu, rw_w0, rw_w2, rw_a0, rw_a2, rw_g2, rw_kk, rw_ka, rw_rk, rw_lnw, rw_lnb,
                  at_qn, at_kn, ml_conv, ml_ib, ml_fb, ml_nw, w_out, norm2_g, mlp_w1, mlp_w2):
    w = w_in[l]
    w_at = w[:, RW_COLS:RW_COLS + AT_COLS]
    w_ml = w[:, RW_COLS + AT_COLS:]
    zero_lora = jnp.zeros((DECAY_LORA, RW_W), F32)
    lora = lambda m: jnp.stack([jnp.concatenate([m[0], zero_lora], 0), jnp.concatenate([zero_lora, m[1]], 0)])
    idx = np.arange(RW_W) // HEAD_DIM
    gate_col = np.arange(LANE)[:, None]
    spread = lambda first: gate_col == first + idx[None, :]
    ex = np.stack([np.concatenate([spread(d * ML_HEADS), spread(2 * ML_HEADS + d * ML_HEADS)], axis=1)
                   for d in range(2)])
    wo = w_out[l]
    return dict(
        g1=norm1_g[l][None, :],
        w_rw=w[:, :RW_COLS].astype(BF16),
        w_at=w_at.astype(BF16),
        w_ml=w_ml[:, :4 * ML_W].astype(BF16),
        w_g=jnp.pad(w_ml[:, 4 * ML_W:], ((0, 0), (0, LANE - 4 * ML_HEADS))).astype(BF16),
        mu=rw_mu[l][None, :], w0=rw_w0[l], w2=lora(rw_w2[l]), a0=rw_a0[l], a2=lora(rw_a2[l]), g2=rw_g2[l],
        kk=rw_kk[l][None, :], ka=rw_ka[l][None, :],
        bd=jnp.asarray(idx[:, None] == idx[None, :], BF16),
        rk=rw_rk[l].reshape(1, RW_W), lnw=rw_lnw[l][None, :], lnb=rw_lnb[l][None, :],
        qn=jnp.tile(at_qn[l], 2)[None, :], kn=jnp.tile(at_kn[l], 2)[None, :],
        conv=ml_conv[l],
        gbias=jnp.pad(jnp.concatenate([ml_ib[l].reshape(-1), ml_fb[l].reshape(-1)]),
                      (0, LANE - 4 * ML_HEADS))[None, :],
        nw=ml_nw[l][None, :], ex=jnp.asarray(ex, BF16),
        wo_rw=wo[:RW_W].astype(BF16),
        wo_at=wo[RW_W:RW_W + AT_W].astype(BF16),
        wo_ml=wo[RW_W + AT_W:].astype(BF16),
        g2n=norm2_g[l][None, :], w1=mlp_w1[l].astype(BF16), w2m=mlp_w2[l].astype(BF16),
    )


def _tiles(S):
    return dict(tm=512, ts=512, cps=8, tq=128, tk=1024)


def _trunk(x, layers, final_g):
    B, S, _ = x.shape
    t = _tiles(S)
    xf = x.reshape(B * S, D_MODEL)
    cos, sin = _rope_tables(S)
    gf = final_g[None, :]
    for l, p in enumerate(layers):
        rw, ml, gc, qh, kh, va, vb = _inproj(xf, p, cos, sin, S, t["tm"])
        sh, df, db, gate = _rw_prep(rw, p, B, S, t["ts"])
        yf, yb = _rw_scan(sh, df, db, B, S, t["cps"])
        ao = _attn(qh, kh, va, vb, B, S, t["tq"], t["tk"])
        mq, mk = _ml_prep(ml, p["conv"], B, S, t["ts"])
        hf, hb = _ml_scan(mq, mk, ml, gc, p["gbias"], p["ex"], p["bd"], B, S, t["cps"])
        h = _outproj(xf, yf, yb, sh, gate, df, db, ao, hf, hb, ml, p, t["tm"])
        xf = _mlp(h, p["g2n"], p["w1"], p["w2m"], gf, l == len(layers) - 1, t["tm"])
    return xf.reshape(B, S, D_MODEL)


def kernel(x_prompt, x_sample, norm1_g, w_in, rw_mu, rw_w0, rw_w2, rw_a0, rw_a2, rw_g2, rw_kk, rw_ka, rw_rk,
           rw_lnw, rw_lnb, at_qn, at_kn, ml_conv, ml_ib, ml_fb, ml_nw, w_out, norm2_g, mlp_w1, mlp_w2, final_g):
    layers = [_layer_params(l, norm1_g, w_in, rw_mu, rw_w0, rw_w2, rw_a0, rw_a2, rw_g2, rw_kk, rw_ka, rw_rk,
                            rw_lnw, rw_lnb, at_qn, at_kn, ml_conv, ml_ib, ml_fb, ml_nw, w_out, norm2_g,
                            mlp_w1, mlp_w2) for l in range(DEPTH)]
    return _trunk(x_prompt, layers, final_g), _trunk(x_sample, layers, final_g)
```

```python
import functools

import numpy as np
import jax
import jax.numpy as jnp
from jax import lax
from jax.experimental import pallas as pl
from jax.experimental.pallas import tpu as pltpu

F32 = jnp.float32
BF16 = jnp.bfloat16

D_MODEL = 1024
DEPTH = 2
GRID_W = 64
HEAD_DIM = 64
RW_W = 256
AT_HEADS = 8
AT_KV_HEADS = 2
AT_GROUP = 4
AT_W = 512
AT_KV_W = 128
ML_HEADS = 4
ML_W = 256
DECAY_LORA = 64
AAA_LORA = 64
GATE_LORA = 128
D_FF = 4096
ROPE_THETA = 10000.0
NORM_EPS = 1e-6
RW_LN_EPS = 64e-5
RW_COLS = 1152
AT_COLS = 768
ML_COLS = 1040
CHUNK = 64
LANE = 128
HALO = 16
VMEM_LIMIT = 56 * 1024 * 1024
Q_SCALE = HEAD_DIM ** -0.5 * float(np.log2(np.e))


def _params(sem):
    return pltpu.CompilerParams(dimension_semantics=sem, vmem_limit_bytes=VMEM_LIMIT)


def _dot(a, b):
    return jnp.dot(a.astype(BF16), b.astype(BF16), preferred_element_type=F32)


def _dot_nt(a, b):
    return lax.dot_general(a.astype(BF16), b.astype(BF16), (((1,), (1,)), ((), ())), preferred_element_type=F32)


def _dot_tn(a, b):
    return lax.dot_general(a.astype(BF16), b.astype(BF16), (((0,), (0,)), ((), ())), preferred_element_type=F32)


def _split3(x):
    hi = x.astype(BF16)
    r = x - hi.astype(F32)
    mid = r.astype(BF16)
    lo = (r - mid.astype(F32)).astype(BF16)
    return hi, mid, lo


def _dot_sel(x, sel):
    hi, mid, lo = _split3(x)
    d = lambda a: jnp.dot(a, sel, preferred_element_type=F32)
    return d(hi) + d(mid) + d(lo)


def _sel_dot(sel, x):
    hi, mid, lo = _split3(x)
    d = lambda a: jnp.dot(sel, a, preferred_element_type=F32)
    return d(hi) + d(mid) + d(lo)


def _rms(x, g):
    return x * lax.rsqrt(jnp.mean(x * x, axis=-1, keepdims=True) + NORM_EPS) * g


def _softplus(z):
    return jnp.maximum(z, 0.0) + jnp.log1p(jnp.exp(-jnp.abs(z)))


def _sigmoid(z):
    return 1.0 / (1.0 + jnp.exp(-z))


def _const_spec(shape):
    nd = len(shape)
    return pl.BlockSpec(shape, lambda *_: (0,) * nd)


def _neighbours(ext, first, last):
    n = ext.shape[0] - 2 * HALO
    row = lax.broadcasted_iota(jnp.int32, (ext.shape[0], 1), 0)
    pad = jnp.logical_or(jnp.logical_and(row == HALO - 1, first), jnp.logical_and(row == HALO + n, last))
    ext = jnp.where(pad, 0.0, ext)
    rows = slice(HALO, HALO + n)
    return ext[rows], pltpu.roll(ext, 1, 0)[rows], pltpu.roll(ext, ext.shape[0] - 1, 0)[rows]


def _rope(y, cos, sin):
    lane = lax.broadcasted_iota(jnp.int32, y.shape, 1)
    rot = jnp.where(lane % 32 < 16, -pltpu.roll(y, LANE - 16, 1), pltpu.roll(y, 16, 1))
    return y * cos + rot * sin


def _half_lane_ms(x, low):
    sq = x * x
    return jnp.where(low, jnp.sum(jnp.where(low, sq, 0.0), axis=-1, keepdims=True),
                     jnp.sum(jnp.where(low, 0.0, sq), axis=-1, keepdims=True)) * (1.0 / HEAD_DIM)


def _front_kernel(x_ref, xp_ref, xn_ref, g_ref, wrw_ref, wat_ref, wml_ref, wg_ref, cos_ref, sin_ref, qn_ref, kn_ref,
                  mu_ref, w0_ref, w2_ref, a0_ref, a2_ref, g2_ref, kk_ref, ka_ref, bd_ref, cw_ref,
                  sh_ref, df_ref, db_ref, gate_ref, mq_ref, mk_ref, mvo_ref, gc_ref, qo_ref, ko_ref, va_ref, vb_ref):
    j = pl.program_id(1)
    first, last = j == 0, j == pl.num_programs(1) - 1
    g1 = g_ref[...]
    xb = _rms(x_ref[...], g1).astype(BF16)
    xb_ext = jnp.concatenate([_rms(xp_ref[...], g1).astype(BF16), xb, _rms(xn_ref[...], g1).astype(BF16)], axis=0)

    rw_ext = jnp.dot(xb_ext, wrw_ref[...], preferred_element_type=F32)
    ml_ext = jnp.dot(xb_ext, wml_ref[:, 0:2 * ML_W], preferred_element_type=F32)
    at = jnp.dot(xb, wat_ref[...], preferred_element_type=F32)
    mvo_ref[...] = jnp.dot(xb, wml_ref[:, 2 * ML_W:4 * ML_W], preferred_element_type=F32)
    gc_ref[...] = jnp.dot(xb, wg_ref[...], preferred_element_type=F32)

    x, xp, xn = _neighbours(rw_ext, first, last)
    xs = x + (0.5 * (xp + xn) - x) * mu_ref[...]
    r, k, v = xs[:, 0:RW_W], xs[:, RW_W:2 * RW_W], xs[:, 2 * RW_W:3 * RW_W]
    wd = xs[:, 3 * RW_W:3 * RW_W + LANE]
    ad = xs[:, 3 * RW_W + LANE:3 * RW_W + 2 * LANE]
    gd = xs[:, 3 * RW_W + 2 * LANE:3 * RW_W + 3 * LANE]
    kk = k * kk_ref[...]
    kap = kk / jnp.maximum(jnp.sqrt(_dot_sel(kk * kk, bd_ref[...])), 1e-12)
    sh_ref[:, 0:RW_W] = r
    sh_ref[:, RW_W:2 * RW_W] = v
    sh_ref[:, 2 * RW_W:3 * RW_W] = kap
    gate_ref[...] = _dot(_sigmoid(gd), g2_ref[...])
    tw = jnp.tanh(wd)
    for d, o_ref in ((0, df_ref), (1, db_ref)):
        wl = _dot(tw, w2_ref[d])
        o_ref[:, 0:RW_W] = -jnp.exp(-_softplus(-(w0_ref[d:d + 1, :] + wl)) - 0.5)
        a = _sigmoid(a0_ref[d:d + 1, :] + _dot(ad, a2_ref[d]))
        o_ref[:, RW_W:2 * RW_W] = k * (1.0 + (a - 1.0) * ka_ref[...])
        o_ref[:, 2 * RW_W:3 * RW_W] = a

    x, xp, xn = _neighbours(ml_ext, first, last)
    y = cw_ref[0:1, :] * xp + cw_ref[1:2, :] * x + cw_ref[2:3, :] * xn
    y = y * _sigmoid(y)
    mq_ref[...] = y[:, 0:ML_W]
    mk_ref[...] = y[:, ML_W:2 * ML_W] * (HEAD_DIM ** -0.5)

    cos, sin = cos_ref[...], sin_ref[...]
    low = lax.broadcasted_iota(jnp.int32, (at.shape[0], LANE), 1) < HEAD_DIM
    for pair in range(AT_HEADS // 2):
        x = at[:, pair * LANE:(pair + 1) * LANE]
        y = _rope(x * lax.rsqrt(_half_lane_ms(x, low) + NORM_EPS) * qn_ref[...], cos, sin) * Q_SCALE
        swapped = pltpu.roll(y, HEAD_DIM, 1)
        if 2 * pair < AT_GROUP:
            even, odd = jnp.where(low, y, 0.0), jnp.where(low, swapped, 0.0)
        else:
            even, odd = jnp.where(low, 0.0, swapped), jnp.where(low, 0.0, y)
        qo_ref[:, 2 * pair * LANE:(2 * pair + 1) * LANE] = even.astype(BF16)
        qo_ref[:, (2 * pair + 1) * LANE:(2 * pair + 2) * LANE] = odd.astype(BF16)
    kx = at[:, AT_W:AT_W + LANE]
    ko_ref[...] = _rope(kx * lax.rsqrt(_half_lane_ms(kx, low) + NORM_EPS) * kn_ref[...], cos, sin).astype(BF16)
    v = at[:, AT_W + LANE:AT_W + 2 * LANE]
    va_ref[...] = jnp.where(low, v, 1.0).astype(BF16)
    vb_ref[...] = jnp.where(low, 1.0, v).astype(BF16)


def _front(xf, p, cos, sin, B, S, tm):
    T = B * S
    n = S // tm
    per = tm // HALO
    f32_w = (3 * RW_W, 3 * RW_W, 3 * RW_W, RW_W, ML_W, ML_W, 2 * ML_W, LANE)
    bf16_w = (AT_HEADS * LANE, LANE, LANE, LANE)
    tok = lambda w: pl.BlockSpec((tm, w), lambda b, j: (b * n + j, 0))
    before = pl.BlockSpec((HALO, D_MODEL), lambda b, j: (jnp.maximum((b * n + j) * per - 1, 0), 0))
    after = pl.BlockSpec((HALO, D_MODEL), lambda b, j: (jnp.minimum((b * n + j + 1) * per, T // HALO - 1), 0))
    pos = pl.BlockSpec((tm, LANE), lambda b, j: (j, 0))
    return pl.pallas_call(
        _front_kernel,
        grid=(B, n),
        in_specs=[tok(D_MODEL), before, after, _const_spec((1, D_MODEL)), _const_spec((D_MODEL, RW_COLS)),
                  _const_spec((D_MODEL, AT_COLS)), _const_spec((D_MODEL, 4 * ML_W)), _const_spec((D_MODEL, LANE)),
                  pos, pos, _const_spec((1, LANE)), _const_spec((1, LANE)),
                  _const_spec((1, RW_COLS)), _const_spec((2, RW_W)), _const_spec((2, LANE, RW_W)),
                  _const_spec((2, RW_W)), _const_spec((2, LANE, RW_W)), _const_spec((GATE_LORA, RW_W)),
                  _const_spec((1, RW_W)), _const_spec((1, RW_W)), _const_spec((RW_W, RW_W)),
                  _const_spec((3, 2 * ML_W))],
        out_specs=[tok(w) for w in f32_w + bf16_w],
        out_shape=[jax.ShapeDtypeStruct((T, w), F32) for w in f32_w]
        + [jax.ShapeDtypeStruct((T, w), BF16) for w in bf16_w],
        compiler_params=_params(("parallel", "parallel")),
        name="front",
    )(xf, xf, xf, p["g1"], p["w_rw"], p["w_at"], p["w_ml"], p["w_g"], cos, sin, p["qn"], p["kn"],
      p["mu"], p["w0"], p["w2"], p["a0"], p["a2"], p["g2"], p["kk"], p["ka"], p["bd"], p["conv"])


def _head_blockdiag(y):
    lane = lax.broadcasted_iota(jnp.int32, y.shape, 1) // HEAD_DIM
    return jnp.concatenate([jnp.where(lane == h, y, 0.0) for h in range(4)], axis=0)


def _hmm(x, y):
    return _dot(x, _head_blockdiag(y))


def _head_diag_blocks(f):
    lane = lax.broadcasted_iota(jnp.int32, (HEAD_DIM, 4 * HEAD_DIM), 1) // HEAD_DIM
    out = jnp.where(lane == 0, f[0:HEAD_DIM, :], 0.0)
    for h in range(1, 4):
        out = out + jnp.where(lane == h, f[h * HEAD_DIM:(h + 1) * HEAD_DIM, :], 0.0)
    return out


def _in_lockstep(gens):
    out = [None] * len(gens)
    live = list(enumerate(gens))
    while live:
        still = []
        for i, g in live:
            try:
                next(g)
                still.append((i, g))
            except StopIteration as done:
                out[i] = done.value
        live = still
    return out


def _rw_chunk(sh, dd, rev):
    L = CHUNK
    r, v, kap = sh[:, 0:RW_W], sh[:, RW_W:2 * RW_W], sh[:, 2 * RW_W:3 * RW_W]
    lw, kd, a = dd[:, 0:RW_W], dd[:, RW_W:2 * RW_W], dd[:, 2 * RW_W:3 * RW_W]
    t_i = lax.broadcasted_iota(jnp.int32, (L, RW_W), 0)
    s_i = lax.broadcasted_iota(jnp.int32, (L, RW_W), 1) % L
    tt = lax.broadcasted_iota(jnp.int32, (L, L), 0)
    ss = lax.broadcasted_iota(jnp.int32, (L, L), 1)
    if rev:
        tri = (ss >= tt).astype(BF16)
        strict, incl = s_i > t_i, s_i >= t_i
    else:
        tri = (ss <= tt).astype(BF16)
        strict, incl = s_i < t_i, s_i <= t_i
    cs = _sel_dot(tri, lw)
    yield
    last = cs[0:1, :] if rev else cs[L - 1:L, :]
    w_in, w_inv, w_ex, w_rem = jnp.exp(cs), jnp.exp(-cs), jnp.exp(cs - lw), jnp.exp(last - cs)
    b = kap * a
    rt, kt, bt, kb = r * w_in, kd * w_inv, b * w_inv, kap * w_ex
    bh, kh = b * w_rem, kd * w_rem
    gram = _dot_nt(jnp.concatenate([kb, rt], axis=0),
                   jnp.concatenate([_head_blockdiag(bt), _head_blockdiag(kt)], axis=0))
    yield
    a_b = jnp.where(strict, gram[0:L, 0:RW_W], 0.0)
    a_k = jnp.where(strict, gram[0:L, RW_W:2 * RW_W], 0.0)
    g_b = jnp.where(incl, gram[L:2 * L, 0:RW_W], 0.0)
    g_k = jnp.where(incl, gram[L:2 * L, RW_W:2 * RW_W], 0.0)
    blk16 = (t_i // 16) == (s_i // 16)
    blk32 = (t_i // 32) == (s_i // 32)
    eye = jnp.where(t_i == s_i, 1.0, 0.0)
    n1 = jnp.where(blk16, -a_b, 0.0)
    n2 = _hmm(n1, n1)
    akv = _hmm(a_k, v)
    gkv = _hmm(g_k, v)
    yield
    n4 = _hmm(n2, n2)
    x = eye + n1 + n2 + _hmm(n1, n2)
    yield
    n8 = _hmm(n4, n4)
    x = x + _hmm(x, n4)
    yield
    x = x + _hmm(x, n8)
    yield
    e = _hmm(x, jnp.where(blk32 & jnp.logical_not(blk16), a_b, 0.0))
    yield
    x = x - _hmm(e, x)
    yield
    e = _hmm(x, jnp.where(blk32, 0.0, a_b))
    yield
    tinv = x - _hmm(e, x)
    yield
    p1 = _hmm(tinv, kb)
    p2 = _hmm(tinv, akv)
    yield
    q = rt - _hmm(g_b, p1)
    z = gkv - _hmm(g_b, p2)
    row = lax.broadcasted_iota(jnp.int32, (HEAD_DIM, RW_W), 0)
    col = lax.broadcasted_iota(jnp.int32, (HEAD_DIM, RW_W), 1) % HEAD_DIM
    m_t = jnp.where(row == col, jnp.exp(last), 0.0) - _head_diag_blocks(_dot_tn(bh, p1))
    n_t = _head_diag_blocks(_dot_tn(jnp.concatenate([kh, -bh], axis=0), jnp.concatenate([v, p2], axis=0)))
    return jnp.concatenate([q, m_t], axis=0), z, n_t


def _rwscan_kernel(shf_ref, df_ref, shb_ref, db_ref, yf_ref, yb_ref, hf_scr, hb_scr, *, cps):
    @pl.when(pl.program_id(1) == 0)
    def _():
        hf_scr[...] = jnp.zeros_like(hf_scr)
        hb_scr[...] = jnp.zeros_like(hb_scr)

    fwd_lo = [ci * CHUNK for ci in range(cps)]
    bwd_lo = fwd_lo[::-1]
    local = _in_lockstep(
        [_rw_chunk(shf_ref[lo:lo + CHUNK, :], df_ref[lo:lo + CHUNK, :], False) for lo in fwd_lo]
        + [_rw_chunk(shb_ref[lo:lo + CHUNK, :], db_ref[lo:lo + CHUNK, :], True) for lo in bwd_lo])
    state = [hf_scr[...], hb_scr[...]]
    for ci in range(cps):
        for d, (los, y_ref) in enumerate(((fwd_lo, yf_ref), (bwd_lo, yb_ref))):
            qm, z, n_t = local[d * cps + ci]
            both = _hmm(qm, state[d])
            y_ref[los[ci]:los[ci] + CHUNK, :] = both[0:CHUNK] + z
            state[d] = both[CHUNK:2 * CHUNK] + n_t
    hf_scr[...], hb_scr[...] = state


def _rw_scan(sh, df, db, B, S, cps):
    T = B * S
    rows = cps * CHUNK
    n = S // rows
    fwd = lambda b, c: (b * n + c, 0)
    bwd = lambda b, c: (b * n + n - 1 - c, 0)
    return pl.pallas_call(
        functools.partial(_rwscan_kernel, cps=cps),
        grid=(B, n),
        in_specs=[pl.BlockSpec((rows, 3 * RW_W), fwd), pl.BlockSpec((rows, 3 * RW_W), fwd),
                  pl.BlockSpec((rows, 3 * RW_W), bwd), pl.BlockSpec((rows, 3 * RW_W), bwd)],
        out_specs=[pl.BlockSpec((rows, RW_W), fwd), pl.BlockSpec((rows, RW_W), bwd)],
        out_shape=[jax.ShapeDtypeStruct((T, RW_W), F32)] * 2,
        scratch_shapes=[pltpu.VMEM((HEAD_DIM, RW_W), F32)] * 2,
        compiler_params=_params(("parallel", "arbitrary")),
        name="rw_scan",
    )(sh, df, sh, db)


def _attn_kernel(q_ref, k_ref, va_ref, vb_ref, o_ref, s_scr, *, tq, tk, n_k):
    q = jnp.concatenate([q_ref[:, h * LANE:(h + 1) * LANE] for h in range(AT_HEADS)], axis=0)
    m_rows = AT_HEADS * tq
    half = m_rows // 2

    def scores(c):
        return lax.dot_general(q, k_ref[c * tk:(c + 1) * tk, :], (((1,), (1,)), ((), ())),
                               preferred_element_type=F32)

    def update(c, s, carry):
        m, acc = carry
        m_new = jnp.maximum(m, jnp.max(s, axis=-1, keepdims=True))
        p = jnp.exp2((s - m_new).astype(BF16))
        keys = slice(c * tk, (c + 1) * tk)
        pv = jnp.concatenate([jnp.dot(p[0:half], va_ref[keys, :], preferred_element_type=F32),
                              jnp.dot(p[half:m_rows], vb_ref[keys, :], preferred_element_type=F32)], axis=0)
        return m_new, jnp.exp2(m - m_new) * acc + pv

    carry = (jnp.full((m_rows, 1), -jnp.inf, F32), jnp.zeros((m_rows, LANE), F32))
    s_scr[0] = scores(0)
    for c in range(n_k):
        if c + 1 < n_k:
            s_scr[(c + 1) % 2] = scores(c + 1)
        carry = update(c, s_scr[c % 2], carry)
    acc = carry[1]
    o = acc / pltpu.roll(acc, HEAD_DIM, 1)
    low = lax.broadcasted_iota(jnp.int32, (tq, LANE), 1) < HEAD_DIM
    for pair in range(AT_HEADS // 2):
        even, odd = o[2 * pair * tq:(2 * pair + 1) * tq, :], o[(2 * pair + 1) * tq:(2 * pair + 2) * tq, :]
        if 2 * pair < AT_GROUP:
            both = jnp.where(low, even, pltpu.roll(odd, HEAD_DIM, 1))
        else:
            both = jnp.where(low, pltpu.roll(even, HEAD_DIM, 1), odd)
        o_ref[:, pair * LANE:(pair + 1) * LANE] = both.astype(BF16)


def _attn(qh, kh, va, vb, B, S, tq, tk):
    T = B * S
    n = S // tq
    seq = pl.BlockSpec((S, LANE), lambda b, i: (b, 0))
    return pl.pallas_call(
        functools.partial(_attn_kernel, tq=tq, tk=tk, n_k=S // tk),
        grid=(B, n),
        in_specs=[pl.BlockSpec((tq, AT_HEADS * LANE), lambda b, i: (b * n + i, 0)),
                  seq, seq, seq],
        out_specs=pl.BlockSpec((tq, AT_W), lambda b, i: (b * n + i, 0)),
        out_shape=jax.ShapeDtypeStruct((T, AT_W), BF16),
        scratch_shapes=[pltpu.VMEM((2, AT_HEADS * tq, tk), F32)],
        compiler_params=_params(("parallel", "parallel")),
        name="attn",
    )(qh, kh, va, vb)


def _scan_max(x, rev):
    n = x.shape[0]
    row = lax.broadcasted_iota(jnp.int32, x.shape, 0)
    step = 1
    while step < n:
        if rev:
            shifted = jnp.where(row >= n - step, -jnp.inf, pltpu.roll(x, n - step, 0))
        else:
            shifted = jnp.where(row < step, -jnp.inf, pltpu.roll(x, step, 0))
        x = jnp.maximum(x, shifted)
        step *= 2
    return x


def _ml_chunk_local(q, k, v, ig, lf, bd, rev):
    L = CHUNK
    t_i = lax.broadcasted_iota(jnp.int32, (L, ML_W), 0)
    s_i = lax.broadcasted_iota(jnp.int32, (L, ML_W), 1) % L
    tt = lax.broadcasted_iota(jnp.int32, (L, L), 0)
    ss = lax.broadcasted_iota(jnp.int32, (L, L), 1)
    tri = ((ss >= tt) if rev else (ss <= tt)).astype(BF16)
    causal = (s_i >= t_i) if rev else (s_i <= t_i)
    bcol = _sel_dot(tri, lf)
    qk = _dot_nt(q, _head_blockdiag(k))
    yield
    b_last = bcol[0:1, :] if rev else bcol[L - 1:L, :]
    c = ig - bcol
    m_loc = bcol + _scan_max(c, rev)
    c_row = jnp.sum(jnp.where(t_i == s_i, c, 0.0), axis=0, keepdims=True)
    pb = (jnp.exp(jnp.where(causal, bcol + c_row - m_loc, -jnp.inf)) * qk).astype(BF16)
    g = b_last - bcol + ig
    mg = jnp.max(g, axis=0, keepdims=True)
    wgt = jnp.exp(g - mg)
    num = jnp.dot(pb, _head_blockdiag(v).astype(BF16), preferred_element_type=F32)
    den = jnp.dot(pb, bd, preferred_element_type=F32)
    kc = _head_diag_blocks(_dot_tn(k, wgt * v))
    nc = jnp.sum(wgt * k, axis=0, keepdims=True)
    return dict(q=q, bcol=bcol, b_last=b_last, m_loc=m_loc, num=num, den=den, mg=mg, kc=kc, nc=nc)


def _ml_chunk_combine(loc, ct, n, m, bd):
    q = loc["q"]
    inter = loc["bcol"] + m
    m_t = jnp.maximum(loc["m_loc"], inter)
    e_loc, e_int = jnp.exp(loc["m_loc"] - m_t), jnp.exp(inter - m_t)
    num = e_loc * loc["num"] + e_int * _hmm(q, ct)
    den = e_loc * loc["den"] + e_int * jnp.dot((q * n).astype(BF16), bd, preferred_element_type=F32)
    h = num / jnp.maximum(jnp.abs(den), jnp.exp(-m_t))
    m_new = jnp.maximum(loc["b_last"] + m, loc["mg"])
    a1 = jnp.exp(loc["b_last"] + m - m_new)
    a2 = jnp.exp(loc["mg"] - m_new)
    return h, a1 * ct + a2 * loc["kc"], a1 * n + a2 * loc["nc"], m_new


def _mlscan_kernel(qf_ref, kf_ref, vf_ref, gf_ref, qb_ref, kb_ref, vb_ref, gb_ref, bias_ref, ex_ref, bd_ref,
                   hf_ref, hb_ref, ct_scr, n_scr, m_scr, *, cps):
    @pl.when(pl.program_id(1) == 0)
    def _():
        ct_scr[...] = jnp.zeros_like(ct_scr)
        n_scr[...] = jnp.zeros_like(n_scr)
        m_scr[...] = jnp.zeros_like(m_scr)

    bd = bd_ref[...]
    dirs = ((0, False, qf_ref, kf_ref, vf_ref, gf_ref, hf_ref), (1, True, qb_ref, kb_ref, vb_ref, gb_ref, hb_ref))
    gens = []
    for d, rev, q_ref, k_ref, v_ref, g_ref, h_ref in dirs:
        gates = g_ref[...] + bias_ref[...]
        log_sig = jnp.minimum(gates, 0.0) - jnp.log1p(jnp.exp(-jnp.abs(gates)))
        lane = lax.broadcasted_iota(jnp.int32, gates.shape, 1)
        spread = _dot_sel(jnp.where(lane < 2 * ML_HEADS, gates, log_sig), ex_ref[d])
        order = [(cps - 1 - ci) * CHUNK for ci in range(cps)] if rev else [ci * CHUNK for ci in range(cps)]
        gens += [_ml_chunk_local(q_ref[lo:lo + CHUNK, :], k_ref[lo:lo + CHUNK, :], v_ref[lo:lo + CHUNK, :],
                                 spread[lo:lo + CHUNK, 0:ML_W], spread[lo:lo + CHUNK, ML_W:2 * ML_W], bd, rev)
                 for lo in order]
    local = _in_lockstep(gens)
    state = [(ct_scr[d], n_scr[d], m_scr[d]) for d in range(2)]
    for ci in range(cps):
        for d, rev, _, _, _, _, h_ref in dirs:
            lo = (cps - 1 - ci) * CHUNK if rev else ci * CHUNK
            out, *state[d] = _ml_chunk_combine(local[d * cps + ci], *state[d], bd)
            h_ref[lo:lo + CHUNK, :] = out
    for d in range(2):
        ct_scr[d], n_scr[d], m_scr[d] = state[d]


def _ml_scan(mq, mk, ml, gc, bias, ex, bd, B, S, cps):
    T = B * S
    rows = cps * CHUNK
    n = S // rows
    fwd = lambda b, c: (b * n + c, 0)
    bwd = lambda b, c: (b * n + n - 1 - c, 0)
    fwd_v, bwd_v = fwd, bwd
    tile = lambda im: pl.BlockSpec((rows, ML_W), im)
    gate = lambda im: pl.BlockSpec((rows, LANE), im)
    return pl.pallas_call(
        functools.partial(_mlscan_kernel, cps=cps),
        grid=(B, n),
        in_specs=[tile(fwd), tile(fwd), tile(fwd_v), gate(fwd), tile(bwd), tile(bwd), tile(bwd_v), gate(bwd),
                  _const_spec((1, LANE)), _const_spec((2, LANE, 2 * ML_W)), _const_spec((ML_W, ML_W))],
        out_specs=[tile(fwd), tile(bwd)],
        out_shape=[jax.ShapeDtypeStruct((T, ML_W), F32)] * 2,
        scratch_shapes=[pltpu.VMEM((2, HEAD_DIM, ML_W), F32), pltpu.VMEM((2, 1, ML_W), F32),
                        pltpu.VMEM((2, 1, ML_W), F32)],
        compiler_params=_params(("parallel", "arbitrary")),
        name="ml_scan",
    )(mq, mk, ml, gc, mq, mk, ml, gc, bias, ex, bd)


def _outproj_kernel(x_ref, yf_ref, yb_ref, r_ref, v_ref, g_ref, df_ref, db_ref, ao_ref, hf_ref, hb_ref, og_ref,
                    rk_ref, lnw_ref, lnb_ref, nw_ref, bd_ref, worw_ref, woat_ref, woml_ref, o_ref):
    bd = bd_ref[...]
    inv = 1.0 / HEAD_DIM
    out = x_ref[...] + jnp.dot(ao_ref[...], woat_ref[...], preferred_element_type=F32)
    wkv = yf_ref[...] + yb_ref[...]
    hm = hf_ref[...] + hb_ref[...]
    mean = _dot_sel(wkv, bd) * inv
    bonus = _dot_sel(r_ref[...] * (df_ref[...] + db_ref[...]) * rk_ref[...], bd) * v_ref[...]
    hms = _dot_sel(hm * hm, bd) * inv
    y_ml = _sigmoid(og_ref[...]) * (hm * lax.rsqrt(hms + NORM_EPS) * nw_ref[...])
    out = out + _dot(y_ml, woml_ref[...])
    dev = wkv - mean
    var = _dot_sel(dev * dev, bd) * inv
    y = dev * lax.rsqrt(var + RW_LN_EPS) * lnw_ref[...] + lnb_ref[...]
    y_rw = (y + bonus) * g_ref[...]
    o_ref[...] = out + _dot(y_rw, worw_ref[...])


def _outproj(xf, yf, yb, sh, gate, df, db, ao, hf, hb, ml, p, tm):
    T = xf.shape[0]
    tok = lambda w, c=0: pl.BlockSpec((tm, w), lambda i: (i, c))
    vec = _const_spec((1, RW_W))
    return pl.pallas_call(
        _outproj_kernel,
        grid=(T // tm,),
        in_specs=[tok(D_MODEL), tok(RW_W), tok(RW_W), tok(RW_W, 0), tok(RW_W, 1), tok(RW_W), tok(RW_W, 1), tok(RW_W, 1),
                  tok(AT_W), tok(ML_W), tok(ML_W), tok(ML_W, 1),
                  vec, vec, vec, vec, _const_spec((RW_W, RW_W)),
                  _const_spec((RW_W, D_MODEL)), _const_spec((AT_W, D_MODEL)), _const_spec((ML_W, D_MODEL))],
        out_specs=tok(D_MODEL),
        out_shape=jax.ShapeDtypeStruct((T, D_MODEL), F32),
        compiler_params=_params(("parallel",)),
        name="outproj",
    )(xf, yf, yb, sh, sh, gate, df, db, ao, hf, hb, ml, p["rk"], p["lnw"], p["lnb"], p["nw"], p["bd"],
      p["wo_rw"], p["wo_at"], p["wo_ml"])


def _mlp_kernel(h_ref, g_ref, w1_ref, w2_ref, gf_ref, o_ref, *, final):
    h = h_ref[...]
    u = jnp.dot(_rms(h, g_ref[...]).astype(BF16), w1_ref[...], preferred_element_type=F32)
    u = jnp.square(jnp.maximum(u, 0.0)).astype(BF16)
    out = h + jnp.dot(u, w2_ref[...], preferred_element_type=F32)
    if final:
        out = _rms(out, gf_ref[...])
    o_ref[...] = out


def _mlp(h, g, w1, w2, gf, final, tm):
    T = h.shape[0]
    once = pl.Buffered(1)
    return pl.pallas_call(
        functools.partial(_mlp_kernel, final=final),
        grid=(T // tm,),
        in_specs=[pl.BlockSpec((tm, D_MODEL), lambda i: (i, 0)), _const_spec((1, D_MODEL)),
                  pl.BlockSpec((D_MODEL, D_FF), lambda i: (0, 0), pipeline_mode=once),
                  pl.BlockSpec((D_FF, D_MODEL), lambda i: (0, 0), pipeline_mode=once),
                  _const_spec((1, D_MODEL))],
        out_specs=pl.BlockSpec((tm, D_MODEL), lambda i: (i, 0)),
        out_shape=jax.ShapeDtypeStruct((T, D_MODEL), F32),
        compiler_params=_params(("parallel",)),
        name="mlp_final" if final else "mlp",
    )(h, g, w1, w2, gf)


def _rope_tables(seq_len):
    t = np.arange(seq_len)
    n_freq = HEAD_DIM // 4
    inv = ROPE_THETA ** (-jnp.arange(n_freq, dtype=F32) / n_freq)
    ang_r = jnp.asarray(t // GRID_W, F32)[:, None] * inv
    ang_c = jnp.asarray(t % GRID_W, F32)[:, None] * inv
    ang = jnp.concatenate([ang_r, ang_r, ang_c, ang_c] * 2, axis=-1)
    return jnp.cos(ang), jnp.sin(ang)


def _layer_params(l, norm1_g, w_in, rw_mu, rw_w0, rw_w2, rw_a0, rw_a2, rw_g2, rw_kk, rw_ka, rw_rk, rw_lnw, rw_lnb,
                  at_qn, at_kn, ml_conv, ml_ib, ml_fb, ml_nw, w_out, norm2_g, mlp_w1, mlp_w2):
    w = w_in[l]
    w_at = w[:, RW_COLS:RW_COLS + AT_COLS]
    w_ml = w[:, RW_COLS + AT_COLS:]
    zero_lora = jnp.zeros((DECAY_LORA, RW_W), F32)
    lora = lambda m: jnp.stack([jnp.concatenate([m[0], zero_lora], 0), jnp.concatenate([zero_lora, m[1]], 0)])
    idx = np.arange(RW_W) // HEAD_DIM
    gate_col = np.arange(LANE)[:, None]
    spread = lambda first: gate_col == first + idx[None, :]
    ex = np.stack([np.concatenate([spread(d * ML_HEADS), spread(2 * ML_HEADS + d * ML_HEADS)], axis=1)
                   for d in range(2)])
    wo = w_out[l]
    return dict(
        g1=norm1_g[l][None, :],
        w_rw=w[:, :RW_COLS].astype(BF16),
        w_at=w_at.astype(BF16),
        w_ml=w_ml[:, :4 * ML_W].astype(BF16),
        w_g=jnp.pad(w_ml[:, 4 * ML_W:], ((0, 0), (0, LANE - 4 * ML_HEADS))).astype(BF16),
        mu=rw_mu[l][None, :], w0=rw_w0[l], w2=lora(rw_w2[l]), a0=rw_a0[l], a2=lora(rw_a2[l]), g2=rw_g2[l],
        kk=rw_kk[l][None, :], ka=rw_ka[l][None, :],
        bd=jnp.asarray(idx[:, None] == idx[None, :], BF16),
        rk=rw_rk[l].reshape(1, RW_W), lnw=rw_lnw[l][None, :], lnb=rw_lnb[l][None, :],
        qn=jnp.tile(at_qn[l], 2)[None, :], kn=jnp.tile(at_kn[l], 2)[None, :],
        conv=ml_conv[l],
        gbias=jnp.pad(jnp.concatenate([ml_ib[l].reshape(-1), ml_fb[l].reshape(-1)]),
                      (0, LANE - 4 * ML_HEADS))[None, :],
        nw=ml_nw[l][None, :], ex=jnp.asarray(ex, BF16),
        wo_rw=wo[:RW_W].astype(BF16),
        wo_at=wo[RW_W:RW_W + AT_W].astype(BF16),
        wo_ml=wo[RW_W + AT_W:].astype(BF16),
        g2n=norm2_g[l][None, :], w1=mlp_w1[l].astype(BF16), w2m=mlp_w2[l].astype(BF16),
    )


def _tiles(S):
    return dict(tm=512, cps=8, tq=128, tk=1024)


def _trunk(x, layers, final_g):
    B, S, _ = x.shape
    t = _tiles(S)
    xf = x.reshape(B * S, D_MODEL)
    cos, sin = _rope_tables(S)
    gf = final_g[None, :]
    for l, p in enumerate(layers):
        sh, df, db, gate, mq, mk, mvo, gc, qh, kh, va, vb = _front(xf, p, cos, sin, B, S, t["tm"])
        yf, yb = _rw_scan(sh, df, db, B, S, t["cps"])
        ao = _attn(qh, kh, va, vb, B, S, t["tq"], t["tk"])
        hf, hb = _ml_scan(mq, mk, mvo, gc, p["gbias"], p["ex"], p["bd"], B, S, t["cps"])
        h = _outproj(xf, yf, yb, sh, gate, df, db, ao, hf, hb, mvo, p, t["tm"])
        xf = _mlp(h, p["g2n"], p["w1"], p["w2m"], gf, l == len(layers) - 1, t["tm"])
    return xf.reshape(B, S, D_MODEL)


def kernel(x_prompt, x_sample, norm1_g, w_in, rw_mu, rw_w0, rw_w2, rw_a0, rw_a2, rw_g2, rw_kk, rw_ka, rw_rk,
           rw_lnw, rw_lnb, at_qn, at_kn, ml_conv, ml_ib, ml_fb, ml_nw, w_out, norm2_g, mlp_w1, mlp_w2, final_g):
    layers = [_layer_params(l, norm1_g, w_in, rw_mu, rw_w0, rw_w2, rw_a0, rw_a2, rw_g2, rw_kk, rw_ka, rw_rk,
                            rw_lnw, rw_lnb, at_qn, at_kn, ml_conv, ml_ib, ml_fb, ml_nw, w_out, norm2_g,
                            mlp_w1, mlp_w2) for l in range(DEPTH)]
    return _trunk(x_prompt, layers, final_g), _trunk(x_sample, layers, final_g)
```

```python
import functools

import numpy as np
import jax
import jax.numpy as jnp
from jax import lax
from jax.experimental import pallas as pl
from jax.experimental.pallas import tpu as pltpu

F32 = jnp.float32
BF16 = jnp.bfloat16

D_MODEL = 1024
DEPTH = 2
GRID_W = 64
HEAD_DIM = 64
RW_W = 256
AT_HEADS = 8
AT_KV_HEADS = 2
AT_GROUP = 4
AT_W = 512
AT_KV_W = 128
ML_HEADS = 4
ML_W = 256
DECAY_LORA = 64
AAA_LORA = 64
GATE_LORA = 128
D_FF = 4096
ROPE_THETA = 10000.0
NORM_EPS = 1e-6
RW_LN_EPS = 64e-5
RW_COLS = 1152
AT_COLS = 768
ML_COLS = 1040
CHUNK = 64
LANE = 128
HALO = 16
VMEM_LIMIT = 56 * 1024 * 1024
Q_SCALE = HEAD_DIM ** -0.5 * float(np.log2(np.e))


def _params(sem):
    return pltpu.CompilerParams(dimension_semantics=sem, vmem_limit_bytes=VMEM_LIMIT)


def _dot(a, b):
    return jnp.dot(a.astype(BF16), b.astype(BF16), preferred_element_type=F32)


def _dot_nt(a, b):
    return lax.dot_general(a.astype(BF16), b.astype(BF16), (((1,), (1,)), ((), ())), preferred_element_type=F32)


def _dot_tn(a, b):
    return lax.dot_general(a.astype(BF16), b.astype(BF16), (((0,), (0,)), ((), ())), preferred_element_type=F32)


def _split3(x):
    hi = x.astype(BF16)
    r = x - hi.astype(F32)
    mid = r.astype(BF16)
    lo = (r - mid.astype(F32)).astype(BF16)
    return hi, mid, lo


def _dot_sel(x, sel):
    hi, mid, lo = _split3(x)
    d = lambda a: jnp.dot(a, sel, preferred_element_type=F32)
    return d(hi) + d(mid) + d(lo)


def _sel_dot(sel, x):
    hi, mid, lo = _split3(x)
    d = lambda a: jnp.dot(sel, a, preferred_element_type=F32)
    return d(hi) + d(mid) + d(lo)


def _rms(x, g):
    return x * lax.rsqrt(jnp.mean(x * x, axis=-1, keepdims=True) + NORM_EPS) * g


def _softplus(z):
    return jnp.maximum(z, 0.0) + jnp.log1p(jnp.exp(-jnp.abs(z)))


def _sigmoid(z):
    return 1.0 / (1.0 + jnp.exp(-z))


def _const_spec(shape):
    nd = len(shape)
    return pl.BlockSpec(shape, lambda *_: (0,) * nd)


def _neighbours(ext, first, last):
    n = ext.shape[0] - 2 * HALO
    row = lax.broadcasted_iota(jnp.int32, (ext.shape[0], 1), 0)
    pad = jnp.logical_or(jnp.logical_and(row == HALO - 1, first), jnp.logical_and(row == HALO + n, last))
    ext = jnp.where(pad, 0.0, ext)
    rows = slice(HALO, HALO + n)
    return ext[rows], pltpu.roll(ext, 1, 0)[rows], pltpu.roll(ext, ext.shape[0] - 1, 0)[rows]


def _rope(y, cos, sin):
    lane = lax.broadcasted_iota(jnp.int32, y.shape, 1)
    rot = jnp.where(lane % 32 < 16, -pltpu.roll(y, LANE - 16, 1), pltpu.roll(y, 16, 1))
    return y * cos + rot * sin


def _half_lane_ms(x, low):
    sq = x * x
    return jnp.where(low, jnp.sum(jnp.where(low, sq, 0.0), axis=-1, keepdims=True),
                     jnp.sum(jnp.where(low, 0.0, sq), axis=-1, keepdims=True)) * (1.0 / HEAD_DIM)


def _front_kernel(x_ref, xp_ref, xn_ref, g_ref, wrw_ref, wat_ref, wml_ref, wg_ref, cos_ref, sin_ref, qn_ref, kn_ref,
                  mu_ref, w0_ref, w2_ref, a0_ref, a2_ref, g2_ref, kk_ref, ka_ref, bd_ref, cw_ref,
                  sh_ref, df_ref, db_ref, gate_ref, mq_ref, mk_ref, mvo_ref, gc_ref, qo_ref, ko_ref, va_ref, vb_ref):
    j = pl.program_id(1)
    first, last = j == 0, j == pl.num_programs(1) - 1
    g1 = g_ref[...]
    xb = _rms(x_ref[...], g1).astype(BF16)
    xb_ext = jnp.concatenate([_rms(xp_ref[...], g1).astype(BF16), xb, _rms(xn_ref[...], g1).astype(BF16)], axis=0)

    rw_ext = jnp.dot(xb_ext, wrw_ref[...], preferred_element_type=F32)
    ml_ext = jnp.dot(xb_ext, wml_ref[:, 0:2 * ML_W], preferred_element_type=F32)
    at = jnp.dot(xb, wat_ref[...], preferred_element_type=F32)
    mvo_ref[...] = jnp.dot(xb, wml_ref[:, 2 * ML_W:4 * ML_W], preferred_element_type=F32)
    gc_ref[...] = jnp.dot(xb, wg_ref[...], preferred_element_type=F32)

    x, xp, xn = _neighbours(rw_ext, first, last)
    xs = x + (0.5 * (xp + xn) - x) * mu_ref[...]
    r, k, v = xs[:, 0:RW_W], xs[:, RW_W:2 * RW_W], xs[:, 2 * RW_W:3 * RW_W]
    wd = xs[:, 3 * RW_W:3 * RW_W + LANE]
    ad = xs[:, 3 * RW_W + LANE:3 * RW_W + 2 * LANE]
    gd = xs[:, 3 * RW_W + 2 * LANE:3 * RW_W + 3 * LANE]
    kk = k * kk_ref[...]
    kap = kk / jnp.maximum(jnp.sqrt(_dot_sel(kk * kk, bd_ref[...])), 1e-12)
    sh_ref[:, 0:RW_W] = r
    sh_ref[:, RW_W:2 * RW_W] = v
    sh_ref[:, 2 * RW_W:3 * RW_W] = kap
    gate_ref[...] = _dot(_sigmoid(gd), g2_ref[...])
    tw = jnp.tanh(wd)
    for d, o_ref in ((0, df_ref), (1, db_ref)):
        wl = _dot(tw, w2_ref[d])
        o_ref[:, 0:RW_W] = -jnp.exp(-_softplus(-(w0_ref[d:d + 1, :] + wl)) - 0.5)
        a = _sigmoid(a0_ref[d:d + 1, :] + _dot(ad, a2_ref[d]))
        o_ref[:, RW_W:2 * RW_W] = k * (1.0 + (a - 1.0) * ka_ref[...])
        o_ref[:, 2 * RW_W:3 * RW_W] = a

    x, xp, xn = _neighbours(ml_ext, first, last)
    y = cw_ref[0:1, :] * xp + cw_ref[1:2, :] * x + cw_ref[2:3, :] * xn
    y = y * _sigmoid(y)
    mq_ref[...] = y[:, 0:ML_W]
    mk_ref[...] = y[:, ML_W:2 * ML_W] * (HEAD_DIM ** -0.5)

    cos, sin = cos_ref[...], sin_ref[...]
    low = lax.broadcasted_iota(jnp.int32, (at.shape[0], LANE), 1) < HEAD_DIM
    for pair in range(AT_HEADS // 2):
        x = at[:, pair * LANE:(pair + 1) * LANE]
        y = _rope(x * lax.rsqrt(_half_lane_ms(x, low) + NORM_EPS) * qn_ref[...], cos, sin) * Q_SCALE
        swapped = pltpu.roll(y, HEAD_DIM, 1)
        if 2 * pair < AT_GROUP:
            even, odd = jnp.where(low, y, 0.0), jnp.where(low, swapped, 0.0)
        else:
            even, odd = jnp.where(low, 0.0, swapped), jnp.where(low, 0.0, y)
        qo_ref[:, 2 * pair * LANE:(2 * pair + 1) * LANE] = even.astype(BF16)
        qo_ref[:, (2 * pair + 1) * LANE:(2 * pair + 2) * LANE] = odd.astype(BF16)
    kx = at[:, AT_W:AT_W + LANE]
    ko_ref[...] = _rope(kx * lax.rsqrt(_half_lane_ms(kx, low) + NORM_EPS) * kn_ref[...], cos, sin).astype(BF16)
    v = at[:, AT_W + LANE:AT_W + 2 * LANE]
    va_ref[...] = jnp.where(low, v, 1.0).astype(BF16)
    vb_ref[...] = jnp.where(low, 1.0, v).astype(BF16)


def _front(xf, p, cos, sin, B, S, tm):
    T = B * S
    n = S // tm
    per = tm // HALO
    f32_w = (3 * RW_W, 3 * RW_W, 3 * RW_W, RW_W, ML_W, ML_W, 2 * ML_W, LANE)
    bf16_w = (AT_HEADS * LANE, LANE, LANE, LANE)
    tok = lambda w: pl.BlockSpec((tm, w), lambda b, j: (b * n + j, 0))
    before = pl.BlockSpec((HALO, D_MODEL), lambda b, j: (jnp.maximum((b * n + j) * per - 1, 0), 0))
    after = pl.BlockSpec((HALO, D_MODEL), lambda b, j: (jnp.minimum((b * n + j + 1) * per, T // HALO - 1), 0))
    pos = pl.BlockSpec((tm, LANE), lambda b, j: (j, 0))
    return pl.pallas_call(
        _front_kernel,
        grid=(B, n),
        in_specs=[tok(D_MODEL), before, after, _const_spec((1, D_MODEL)), _const_spec((D_MODEL, RW_COLS)),
                  _const_spec((D_MODEL, AT_COLS)), _const_spec((D_MODEL, 4 * ML_W)), _const_spec((D_MODEL, LANE)),
                  pos, pos, _const_spec((1, LANE)), _const_spec((1, LANE)),
                  _const_spec((1, RW_COLS)), _const_spec((2, RW_W)), _const_spec((2, LANE, RW_W)),
                  _const_spec((2, RW_W)), _const_spec((2, LANE, RW_W)), _const_spec((GATE_LORA, RW_W)),
                  _const_spec((1, RW_W)), _const_spec((1, RW_W)), _const_spec((RW_W, RW_W)),
                  _const_spec((3, 2 * ML_W))],
        out_specs=[tok(w) for w in f32_w + bf16_w],
        out_shape=[jax.ShapeDtypeStruct((T, w), F32) for w in f32_w]
        + [jax.ShapeDtypeStruct((T, w), BF16) for w in bf16_w],
        compiler_params=_params(("parallel", "parallel")),
        name="front",
    )(xf, xf, xf, p["g1"], p["w_rw"], p["w_at"], p["w_ml"], p["w_g"], cos, sin, p["qn"], p["kn"],
      p["mu"], p["w0"], p["w2"], p["a0"], p["a2"], p["g2"], p["kk"], p["ka"], p["bd"], p["conv"])


def _head_blockdiag(y):
    lane = lax.broadcasted_iota(jnp.int32, y.shape, 1) // HEAD_DIM
    return jnp.concatenate([jnp.where(lane == h, y, 0.0) for h in range(4)], axis=0)


def _hmm(x, y):
    return _dot(x, _head_blockdiag(y))


def _hmm2(x1, x2, y):
    both = _hmm(jnp.concatenate([x1, x2], axis=0), y)
    return both[0:x1.shape[0]], both[x1.shape[0]:]


def _head_diag_blocks(f):
    lane = lax.broadcasted_iota(jnp.int32, (HEAD_DIM, 4 * HEAD_DIM), 1) // HEAD_DIM
    out = jnp.where(lane == 0, f[0:HEAD_DIM, :], 0.0)
    for h in range(1, 4):
        out = out + jnp.where(lane == h, f[h * HEAD_DIM:(h + 1) * HEAD_DIM, :], 0.0)
    return out


def _in_lockstep(gens):
    out = [None] * len(gens)
    live = list(enumerate(gens))
    while live:
        still = []
        for i, g in live:
            try:
                next(g)
                still.append((i, g))
            except StopIteration as done:
                out[i] = done.value
        live = still
    return out


def _rw_chunk(sh, dd, rev):
    L = CHUNK
    r, v, kap = sh[:, 0:RW_W], sh[:, RW_W:2 * RW_W], sh[:, 2 * RW_W:3 * RW_W]
    lw, kd, a = dd[:, 0:RW_W], dd[:, RW_W:2 * RW_W], dd[:, 2 * RW_W:3 * RW_W]
    t_i = lax.broadcasted_iota(jnp.int32, (L, RW_W), 0)
    s_i = lax.broadcasted_iota(jnp.int32, (L, RW_W), 1) % L
    tt = lax.broadcasted_iota(jnp.int32, (L, L), 0)
    ss = lax.broadcasted_iota(jnp.int32, (L, L), 1)
    if rev:
        tri = (ss >= tt).astype(BF16)
        strict, incl = s_i > t_i, s_i >= t_i
    else:
        tri = (ss <= tt).astype(BF16)
        strict, incl = s_i < t_i, s_i <= t_i
    cs = _sel_dot(tri, lw)
    yield
    last = cs[0:1, :] if rev else cs[L - 1:L, :]
    w_in, w_inv, w_ex, w_rem = jnp.exp(cs), jnp.exp(-cs), jnp.exp(cs - lw), jnp.exp(last - cs)
    b = kap * a
    rt, kt, bt, kb = r * w_in, kd * w_inv, b * w_inv, kap * w_ex
    bh, kh = b * w_rem, kd * w_rem
    gram = _dot_nt(jnp.concatenate([kb, rt], axis=0),
                   jnp.concatenate([_head_blockdiag(bt), _head_blockdiag(kt)], axis=0))
    yield
    a_b = jnp.where(strict, gram[0:L, 0:RW_W], 0.0)
    a_k = jnp.where(strict, gram[0:L, RW_W:2 * RW_W], 0.0)
    g_b = jnp.where(incl, gram[L:2 * L, 0:RW_W], 0.0)
    g_k = jnp.where(incl, gram[L:2 * L, RW_W:2 * RW_W], 0.0)
    blk16 = (t_i // 16) == (s_i // 16)
    blk32 = (t_i // 32) == (s_i // 32)
    eye = jnp.where(t_i == s_i, 1.0, 0.0)
    n1 = jnp.where(blk16, -a_b, 0.0)
    n2 = _hmm(n1, n1)
    akv, gkv = _hmm2(a_k, g_k, v)
    yield
    n4, n3 = _hmm2(n2, n1, n2)
    x = eye + n1 + n2 + n3
    yield
    n8, xn4 = _hmm2(n4, x, n4)
    x = x + xn4
    yield
    x = x + _hmm(x, n8)
    yield
    e = _hmm(x, jnp.where(blk32 & jnp.logical_not(blk16), a_b, 0.0))
    yield
    x = x - _hmm(e, x)
    yield
    e = _hmm(x, jnp.where(blk32, 0.0, a_b))
    yield
    tinv = x - _hmm(e, x)
    yield
    p1 = _hmm(tinv, kb)
    p2 = _hmm(tinv, akv)
    yield
    q = rt - _hmm(g_b, p1)
    z = gkv - _hmm(g_b, p2)
    row = lax.broadcasted_iota(jnp.int32, (HEAD_DIM, RW_W), 0)
    col = lax.broadcasted_iota(jnp.int32, (HEAD_DIM, RW_W), 1) % HEAD_DIM
    m_t = jnp.where(row == col, jnp.exp(last), 0.0) - _head_diag_blocks(_dot_tn(bh, p1))
    n_t = _head_diag_blocks(_dot_tn(jnp.concatenate([kh, -bh], axis=0), jnp.concatenate([v, p2], axis=0)))
    return jnp.concatenate([q, m_t], axis=0), z, n_t


def _rwscan_kernel(shf_ref, df_ref, shb_ref, db_ref, yf_ref, yb_ref, hf_scr, hb_scr, *, cps):
    @pl.when(pl.program_id(1) == 0)
    def _():
        hf_scr[...] = jnp.zeros_like(hf_scr)
        hb_scr[...] = jnp.zeros_like(hb_scr)

    fwd_lo = [ci * CHUNK for ci in range(cps)]
    bwd_lo = fwd_lo[::-1]
    local = _in_lockstep(
        [_rw_chunk(shf_ref[lo:lo + CHUNK, :], df_ref[lo:lo + CHUNK, :], False) for lo in fwd_lo]
        + [_rw_chunk(shb_ref[lo:lo + CHUNK, :], db_ref[lo:lo + CHUNK, :], True) for lo in bwd_lo])
    state = [hf_scr[...], hb_scr[...]]
    for ci in range(cps):
        for d, (los, y_ref) in enumerate(((fwd_lo, yf_ref), (bwd_lo, yb_ref))):
            qm, z, n_t = local[d * cps + ci]
            both = _hmm(qm, state[d])
            y_ref[los[ci]:los[ci] + CHUNK, :] = both[0:CHUNK] + z
            state[d] = both[CHUNK:2 * CHUNK] + n_t
    hf_scr[...], hb_scr[...] = state


def _rw_scan(sh, df, db, B, S, cps):
    T = B * S
    rows = cps * CHUNK
    n = S // rows
    fwd = lambda b, c: (b * n + c, 0)
    bwd = lambda b, c: (b * n + n - 1 - c, 0)
    return pl.pallas_call(
        functools.partial(_rwscan_kernel, cps=cps),
        grid=(B, n),
        in_specs=[pl.BlockSpec((rows, 3 * RW_W), fwd), pl.BlockSpec((rows, 3 * RW_W), fwd),
                  pl.BlockSpec((rows, 3 * RW_W), bwd), pl.BlockSpec((rows, 3 * RW_W), bwd)],
        out_specs=[pl.BlockSpec((rows, RW_W), fwd), pl.BlockSpec((rows, RW_W), bwd)],
        out_shape=[jax.ShapeDtypeStruct((T, RW_W), F32)] * 2,
        scratch_shapes=[pltpu.VMEM((HEAD_DIM, RW_W), F32)] * 2,
        compiler_params=_params(("parallel", "arbitrary")),
        name="rw_scan",
    )(sh, df, sh, db)


def _attn_kernel(q_ref, k_ref, va_ref, vb_ref, o_ref, s_scr, *, tq, tk, n_k):
    q = jnp.concatenate([q_ref[:, h * LANE:(h + 1) * LANE] for h in range(AT_HEADS)], axis=0)
    m_rows = AT_HEADS * tq
    half = m_rows // 2

    def scores(c):
        return lax.dot_general(q, k_ref[c * tk:(c + 1) * tk, :], (((1,), (1,)), ((), ())),
                               preferred_element_type=F32)

    def update(c, s, carry):
        m, acc = carry
        m_new = jnp.maximum(m, jnp.max(s, axis=-1, keepdims=True))
        p = jnp.exp2((s - m_new).astype(BF16))
        keys = slice(c * tk, (c + 1) * tk)
        pv = jnp.concatenate([jnp.dot(p[0:half], va_ref[keys, :], preferred_element_type=F32),
                              jnp.dot(p[half:m_rows], vb_ref[keys, :], preferred_element_type=F32)], axis=0)
        return m_new, jnp.exp2(m - m_new) * acc + pv

    carry = (jnp.full((m_rows, 1), -jnp.inf, F32), jnp.zeros((m_rows, LANE), F32))
    s_scr[0] = scores(0)
    for c in range(n_k):
        if c + 1 < n_k:
            s_scr[(c + 1) % 2] = scores(c + 1)
        carry = update(c, s_scr[c % 2], carry)
    acc = carry[1]
    o = acc / pltpu.roll(acc, HEAD_DIM, 1)
    low = lax.broadcasted_iota(jnp.int32, (tq, LANE), 1) < HEAD_DIM
    for pair in range(AT_HEADS // 2):
        even, odd = o[2 * pair * tq:(2 * pair + 1) * tq, :], o[(2 * pair + 1) * tq:(2 * pair + 2) * tq, :]
        if 2 * pair < AT_GROUP:
            both = jnp.where(low, even, pltpu.roll(odd, HEAD_DIM, 1))
        else:
            both = jnp.where(low, pltpu.roll(even, HEAD_DIM, 1), odd)
        o_ref[:, pair * LANE:(pair + 1) * LANE] = both.astype(BF16)


def _attn(qh, kh, va, vb, B, S, tq, tk):
    T = B * S
    n = S // tq
    seq = pl.BlockSpec((S, LANE), lambda b, i: (b, 0))
    return pl.pallas_call(
        functools.partial(_attn_kernel, tq=tq, tk=tk, n_k=S // tk),
        grid=(B, n),
        in_specs=[pl.BlockSpec((tq, AT_HEADS * LANE), lambda b, i: (b * n + i, 0)),
                  seq, seq, seq],
        out_specs=pl.BlockSpec((tq, AT_W), lambda b, i: (b * n + i, 0)),
        out_shape=jax.ShapeDtypeStruct((T, AT_W), BF16),
        scratch_shapes=[pltpu.VMEM((2, AT_HEADS * tq, tk), F32)],
        compiler_params=_params(("parallel", "parallel")),
        name="attn",
    )(qh, kh, va, vb)


def _scan_max(x, rev):
    n = x.shape[0]
    row = lax.broadcasted_iota(jnp.int32, x.shape, 0)
    step = 1
    while step < n:
        if rev:
            shifted = jnp.where(row >= n - step, -jnp.inf, pltpu.roll(x, n - step, 0))
        else:
            shifted = jnp.where(row < step, -jnp.inf, pltpu.roll(x, step, 0))
        x = jnp.maximum(x, shifted)
        step *= 2
    return x


def _ml_chunk_local(q, k, v, ig, lf, bd, rev):
    L = CHUNK
    t_i = lax.broadcasted_iota(jnp.int32, (L, ML_W), 0)
    s_i = lax.broadcasted_iota(jnp.int32, (L, ML_W), 1) % L
    tt = lax.broadcasted_iota(jnp.int32, (L, L), 0)
    ss = lax.broadcasted_iota(jnp.int32, (L, L), 1)
    tri = ((ss >= tt) if rev else (ss <= tt)).astype(BF16)
    causal = (s_i >= t_i) if rev else (s_i <= t_i)
    bcol = _sel_dot(tri, lf)
    qk = _dot_nt(q, _head_blockdiag(k))
    yield
    b_last = bcol[0:1, :] if rev else bcol[L - 1:L, :]
    c = ig - bcol
    m_loc = bcol + _scan_max(c, rev)
    c_row = jnp.sum(jnp.where(t_i == s_i, c, 0.0), axis=0, keepdims=True)
    pb = (jnp.exp(jnp.where(causal, bcol + c_row - m_loc, -jnp.inf)) * qk).astype(BF16)
    g = b_last - bcol + ig
    mg = jnp.max(g, axis=0, keepdims=True)
    wgt = jnp.exp(g - mg)
    num = jnp.dot(pb, _head_blockdiag(v).astype(BF16), preferred_element_type=F32)
    den = jnp.dot(pb, bd, preferred_element_type=F32)
    kc = _head_diag_blocks(_dot_tn(k, wgt * v))
    nc = jnp.sum(wgt * k, axis=0, keepdims=True)
    return dict(q=q, bcol=bcol, b_last=b_last, m_loc=m_loc, num=num, den=den, mg=mg, kc=kc, nc=nc)


def _ml_chunk_combine(loc, ct, n, m, bd):
    q = loc["q"]
    inter = loc["bcol"] + m
    m_t = jnp.maximum(loc["m_loc"], inter)
    e_loc, e_int = jnp.exp(loc["m_loc"] - m_t), jnp.exp(inter - m_t)
    num = e_loc * loc["num"] + e_int * _hmm(q, ct)
    den = e_loc * loc["den"] + e_int * jnp.dot((q * n).astype(BF16), bd, preferred_element_type=F32)
    h = num / jnp.maximum(jnp.abs(den), jnp.exp(-m_t))
    m_new = jnp.maximum(loc["b_last"] + m, loc["mg"])
    a1 = jnp.exp(loc["b_last"] + m - m_new)
    a2 = jnp.exp(loc["mg"] - m_new)
    return h, a1 * ct + a2 * loc["kc"], a1 * n + a2 * loc["nc"], m_new


def _mlscan_kernel(qf_ref, kf_ref, vf_ref, gf_ref, qb_ref, kb_ref, vb_ref, gb_ref, bias_ref, ex_ref, bd_ref,
                   hf_ref, hb_ref, ct_scr, n_scr, m_scr, *, cps):
    @pl.when(pl.program_id(1) == 0)
    def _():
        ct_scr[...] = jnp.zeros_like(ct_scr)
        n_scr[...] = jnp.zeros_like(n_scr)
        m_scr[...] = jnp.zeros_like(m_scr)

    bd = bd_ref[...]
    dirs = ((0, False, qf_ref, kf_ref, vf_ref, gf_ref, hf_ref), (1, True, qb_ref, kb_ref, vb_ref, gb_ref, hb_ref))
    gens = []
    for d, rev, q_ref, k_ref, v_ref, g_ref, h_ref in dirs:
        gates = g_ref[...] + bias_ref[...]
        log_sig = jnp.minimum(gates, 0.0) - jnp.log1p(jnp.exp(-jnp.abs(gates)))
        lane = lax.broadcasted_iota(jnp.int32, gates.shape, 1)
        spread = _dot_sel(jnp.where(lane < 2 * ML_HEADS, gates, log_sig), ex_ref[d])
        order = [(cps - 1 - ci) * CHUNK for ci in range(cps)] if rev else [ci * CHUNK for ci in range(cps)]
        gens += [_ml_chunk_local(q_ref[lo:lo + CHUNK, :], k_ref[lo:lo + CHUNK, :], v_ref[lo:lo + CHUNK, :],
                                 spread[lo:lo + CHUNK, 0:ML_W], spread[lo:lo + CHUNK, ML_W:2 * ML_W], bd, rev)
                 for lo in order]
    local = _in_lockstep(gens)
    state = [(ct_scr[d], n_scr[d], m_scr[d]) for d in range(2)]
    for ci in range(cps):
        for d, rev, _, _, _, _, h_ref in dirs:
            lo = (cps - 1 - ci) * CHUNK if rev else ci * CHUNK
            out, *state[d] = _ml_chunk_combine(local[d * cps + ci], *state[d], bd)
            h_ref[lo:lo + CHUNK, :] = out
    for d in range(2):
        ct_scr[d], n_scr[d], m_scr[d] = state[d]


def _ml_scan(mq, mk, ml, gc, bias, ex, bd, B, S, cps):
    T = B * S
    rows = cps * CHUNK
    n = S // rows
    fwd = lambda b, c: (b * n + c, 0)
    bwd = lambda b, c: (b * n + n - 1 - c, 0)
    fwd_v, bwd_v = fwd, bwd
    tile = lambda im: pl.BlockSpec((rows, ML_W), im)
    gate = lambda im: pl.BlockSpec((rows, LANE), im)
    return pl.pallas_call(
        functools.partial(_mlscan_kernel, cps=cps),
        grid=(B, n),
        in_specs=[tile(fwd), tile(fwd), tile(fwd_v), gate(fwd), tile(bwd), tile(bwd), tile(bwd_v), gate(bwd),
                  _const_spec((1, LANE)), _const_spec((2, LANE, 2 * ML_W)), _const_spec((ML_W, ML_W))],
        out_specs=[tile(fwd), tile(bwd)],
        out_shape=[jax.ShapeDtypeStruct((T, ML_W), F32)] * 2,
        scratch_shapes=[pltpu.VMEM((2, HEAD_DIM, ML_W), F32), pltpu.VMEM((2, 1, ML_W), F32),
                        pltpu.VMEM((2, 1, ML_W), F32)],
        compiler_params=_params(("parallel", "arbitrary")),
        name="ml_scan",
    )(mq, mk, ml, gc, mq, mk, ml, gc, bias, ex, bd)


def _outproj_kernel(x_ref, yf_ref, yb_ref, r_ref, v_ref, g_ref, df_ref, db_ref, ao_ref, hf_ref, hb_ref, og_ref,
                    rk_ref, lnw_ref, lnb_ref, nw_ref, bd_ref, worw_ref, woat_ref, woml_ref, o_ref):
    bd = bd_ref[...]
    inv = 1.0 / HEAD_DIM
    out = x_ref[...] + jnp.dot(ao_ref[...], woat_ref[...], preferred_element_type=F32)
    wkv = yf_ref[...] + yb_ref[...]
    hm = hf_ref[...] + hb_ref[...]
    mean = _dot_sel(wkv, bd) * inv
    bonus = _dot_sel(r_ref[...] * (df_ref[...] + db_ref[...]) * rk_ref[...], bd) * v_ref[...]
    hms = _dot_sel(hm * hm, bd) * inv
    y_ml = _sigmoid(og_ref[...]) * (hm * lax.rsqrt(hms + NORM_EPS) * nw_ref[...])
    out = out + _dot(y_ml, woml_ref[...])
    dev = wkv - mean
    var = _dot_sel(dev * dev, bd) * inv
    y = dev * lax.rsqrt(var + RW_LN_EPS) * lnw_ref[...] + lnb_ref[...]
    y_rw = (y + bonus) * g_ref[...]
    o_ref[...] = out + _dot(y_rw, worw_ref[...])


def _outproj(xf, yf, yb, sh, gate, df, db, ao, hf, hb, ml, p, tm):
    T = xf.shape[0]
    tok = lambda w, c=0: pl.BlockSpec((tm, w), lambda i: (i, c))
    vec = _const_spec((1, RW_W))
    return pl.pallas_call(
        _outproj_kernel,
        grid=(T // tm,),
        in_specs=[tok(D_MODEL), tok(RW_W), tok(RW_W), tok(RW_W, 0), tok(RW_W, 1), tok(RW_W), tok(RW_W, 1), tok(RW_W, 1),
                  tok(AT_W), tok(ML_W), tok(ML_W), tok(ML_W, 1),
                  vec, vec, vec, vec, _const_spec((RW_W, RW_W)),
                  _const_spec((RW_W, D_MODEL)), _const_spec((AT_W, D_MODEL)), _const_spec((ML_W, D_MODEL))],
        out_specs=tok(D_MODEL),
        out_shape=jax.ShapeDtypeStruct((T, D_MODEL), F32),
        compiler_params=_params(("parallel",)),
        name="outproj",
    )(xf, yf, yb, sh, sh, gate, df, db, ao, hf, hb, ml, p["rk"], p["lnw"], p["lnb"], p["nw"], p["bd"],
      p["wo_rw"], p["wo_at"], p["wo_ml"])


def _mlp_kernel(h_ref, g_ref, w1_ref, w2_ref, gf_ref, o_ref, *, final):
    h = h_ref[...]
    u = jnp.dot(_rms(h, g_ref[...]).astype(BF16), w1_ref[...], preferred_element_type=F32)
    u = jnp.square(jnp.maximum(u, 0.0)).astype(BF16)
    out = h + jnp.dot(u, w2_ref[...], preferred_element_type=F32)
    if final:
        out = _rms(out, gf_ref[...])
    o_ref[...] = out


def _mlp(h, g, w1, w2, gf, final, tm):
    T = h.shape[0]
    once = pl.Buffered(1)
    return pl.pallas_call(
        functools.partial(_mlp_kernel, final=final),
        grid=(T // tm,),
        in_specs=[pl.BlockSpec((tm, D_MODEL), lambda i: (i, 0)), _const_spec((1, D_MODEL)),
                  pl.BlockSpec((D_MODEL, D_FF), lambda i: (0, 0), pipeline_mode=once),
                  pl.BlockSpec((D_FF, D_MODEL), lambda i: (0, 0), pipeline_mode=once),
                  _const_spec((1, D_MODEL))],
        out_specs=pl.BlockSpec((tm, D_MODEL), lambda i: (i, 0)),
        out_shape=jax.ShapeDtypeStruct((T, D_MODEL), F32),
        compiler_params=_params(("parallel",)),
        name="mlp_final" if final else "mlp",
    )(h, g, w1, w2, gf)


def _rope_tables(seq_len):
    t = np.arange(seq_len)
    n_freq = HEAD_DIM // 4
    inv = ROPE_THETA ** (-jnp.arange(n_freq, dtype=F32) / n_freq)
    ang_r = jnp.asarray(t // GRID_W, F32)[:, None] * inv
    ang_c = jnp.asarray(t % GRID_W, F32)[:, None] * inv
    ang = jnp.concatenate([ang_r, ang_r, ang_c, ang_c] * 2, axis=-1)
    return jnp.cos(ang), jnp.sin(ang)


def _layer_params(l, norm1_g, w_in, rw_mu, rw_w0, rw_w2, rw_a0, rw_a2, rw_g2, rw_kk, rw_ka, rw_rk, rw_lnw, rw_lnb,
                  at_qn, at_kn, ml_conv, ml_ib, ml_fb, ml_nw, w_out, norm2_g, mlp_w1, mlp_w2):
    w = w_in[l]
    w_at = w[:, RW_COLS:RW_COLS + AT_COLS]
    w_ml = w[:, RW_COLS + AT_COLS:]
    zero_lora = jnp.zeros((DECAY_LORA, RW_W), F32)
    lora = lambda m: jnp.stack([jnp.concatenate([m[0], zero_lora], 0), jnp.concatenate([zero_lora, m[1]], 0)])
    idx = np.arange(RW_W) // HEAD_DIM
    gate_col = np.arange(LANE)[:, None]
    spread = lambda first: gate_col == first + idx[None, :]
    ex = np.stack([np.concatenate([spread(d * ML_HEADS), spread(2 * ML_HEADS + d * ML_HEADS)], axis=1)
                   for d in range(2)])
    wo = w_out[l]
    return dict(
        g1=norm1_g[l][None, :],
        w_rw=w[:, :RW_COLS].astype(BF16),
        w_at=w_at.astype(BF16),
        w_ml=w_ml[:, :4 * ML_W].astype(BF16),
        w_g=jnp.pad(w_ml[:, 4 * ML_W:], ((0, 0), (0, LANE - 4 * ML_HEADS))).astype(BF16),
        mu=rw_mu[l][None, :], w0=rw_w0[l], w2=lora(rw_w2[l]), a0=rw_a0[l], a2=lora(rw_a2[l]), g2=rw_g2[l],
        kk=rw_kk[l][None, :], ka=rw_ka[l][None, :],
        bd=jnp.asarray(idx[:, None] == idx[None, :], BF16),
        rk=rw_rk[l].reshape(1, RW_W), lnw=rw_lnw[l][None, :], lnb=rw_lnb[l][None, :],
        qn=jnp.tile(at_qn[l], 2)[None, :], kn=jnp.tile(at_kn[l], 2)[None, :],
        conv=ml_conv[l],
        gbias=jnp.pad(jnp.concatenate([ml_ib[l].reshape(-1), ml_fb[l].reshape(-1)]),
                      (0, LANE - 4 * ML_HEADS))[None, :],
        nw=ml_nw[l][None, :], ex=jnp.asarray(ex, BF16),
        wo_rw=wo[:RW_W].astype(BF16),
        wo_at=wo[RW_W:RW_W + AT_W].astype(BF16),
        wo_ml=wo[RW_W + AT_W:].astype(BF16),
        g2n=norm2_g[l][None, :], w1=mlp_w1[l].astype(BF16), w2m=mlp_w2[l].astype(BF16),
    )


def _tiles(S):
    return dict(tm=512, cps=8, tq=256, tk=1024)


def _trunk(x, layers, final_g):
    B, S, _ = x.shape
    t = _tiles(S)
    xf = x.reshape(B * S, D_MODEL)
    cos, sin = _rope_tables(S)
    gf = final_g[None, :]
    for l, p in enumerate(layers):
        sh, df, db, gate, mq, mk, mvo, gc, qh, kh, va, vb = _front(xf, p, cos, sin, B, S, t["tm"])
        yf, yb = _rw_scan(sh, df, db, B, S, t["cps"])
        ao = _attn(qh, kh, va, vb, B, S, t["tq"], t["tk"])
        hf, hb = _ml_scan(mq, mk, mvo, gc, p["gbias"], p["ex"], p["bd"], B, S, t["cps"])
        h = _outproj(xf, yf, yb, sh, gate, df, db, ao, hf, hb, mvo, p, t["tm"])
        xf = _mlp(h, p["g2n"], p["w1"], p["w2m"], gf, l == len(layers) - 1, t["tm"])
    return xf.reshape(B, S, D_MODEL)


def kernel(x_prompt, x_sample, norm1_g, w_in, rw_mu, rw_w0, rw_w2, rw_a0, rw_a2, rw_g2, rw_kk, rw_ka, rw_rk,
           rw_lnw, rw_lnb, at_qn, at_kn, ml_conv, ml_ib, ml_fb, ml_nw, w_out, norm2_g, mlp_w1, mlp_w2, final_g):
    layers = [_layer_params(l, norm1_g, w_in, rw_mu, rw_w0, rw_w2, rw_a0, rw_a2, rw_g2, rw_kk, rw_ka, rw_rk,
                            rw_lnw, rw_lnb, at_qn, at_kn, ml_conv, ml_ib, ml_fb, ml_nw, w_out, norm2_g,
                            mlp_w1, mlp_w2) for l in range(DEPTH)]
    return _trunk(x_prompt, layers, final_g), _trunk(x_sample, layers, final_g)
```

```python
import functools

import numpy as np
import jax
import jax.numpy as jnp
from jax import lax
from jax.experimental import pallas as pl
from jax.experimental.pallas import tpu as pltpu

F32 = jnp.float32
BF16 = jnp.bfloat16

D_MODEL = 1024
DEPTH = 2
GRID_W = 64
HEAD_DIM = 64
RW_W = 256
AT_HEADS = 8
AT_KV_HEADS = 2
AT_GROUP = 4
AT_W = 512
AT_KV_W = 128
ML_HEADS = 4
ML_W = 256
DECAY_LORA = 64
AAA_LORA = 64
GATE_LORA = 128
D_FF = 4096
ROPE_THETA = 10000.0
NORM_EPS = 1e-6
RW_LN_EPS = 64e-5
RW_COLS = 1152
AT_COLS = 768
ML_COLS = 1040
CHUNK = 64
LANE = 128
HALO = 16
VMEM_LIMIT = 56 * 1024 * 1024
Q_SCALE = HEAD_DIM ** -0.5 * float(np.log2(np.e))


def _params(sem):
    return pltpu.CompilerParams(dimension_semantics=sem, vmem_limit_bytes=VMEM_LIMIT)


def _dot(a, b):
    return jnp.dot(a.astype(BF16), b.astype(BF16), preferred_element_type=F32)


def _dot_nt(a, b):
    return lax.dot_general(a.astype(BF16), b.astype(BF16), (((1,), (1,)), ((), ())), preferred_element_type=F32)


def _dot_tn(a, b):
    return lax.dot_general(a.astype(BF16), b.astype(BF16), (((0,), (0,)), ((), ())), preferred_element_type=F32)


def _split3(x):
    hi = x.astype(BF16)
    r = x - hi.astype(F32)
    mid = r.astype(BF16)
    lo = (r - mid.astype(F32)).astype(BF16)
    return hi, mid, lo


def _dot_sel(x, sel):
    hi, mid, lo = _split3(x)
    d = lambda a: jnp.dot(a, sel, preferred_element_type=F32)
    return d(hi) + d(mid) + d(lo)


def _sel_dot(sel, x):
    hi, mid, lo = _split3(x)
    d = lambda a: jnp.dot(sel, a, preferred_element_type=F32)
    return d(hi) + d(mid) + d(lo)


def _rms(x, g):
    return x * lax.rsqrt(jnp.mean(x * x, axis=-1, keepdims=True) + NORM_EPS) * g


def _softplus(z):
    return jnp.maximum(z, 0.0) + jnp.log1p(jnp.exp(-jnp.abs(z)))


def _sigmoid(z):
    return 1.0 / (1.0 + jnp.exp(-z))


def _const_spec(shape):
    nd = len(shape)
    return pl.BlockSpec(shape, lambda *_: (0,) * nd)


def _neighbours(ext, first, last):
    n = ext.shape[0] - 2 * HALO
    row = lax.broadcasted_iota(jnp.int32, (ext.shape[0], 1), 0)
    pad = jnp.logical_or(jnp.logical_and(row == HALO - 1, first), jnp.logical_and(row == HALO + n, last))
    ext = jnp.where(pad, 0.0, ext)
    rows = slice(HALO, HALO + n)
    return ext[rows], pltpu.roll(ext, 1, 0)[rows], pltpu.roll(ext, ext.shape[0] - 1, 0)[rows]


def _rope(y, cos, sin):
    lane = lax.broadcasted_iota(jnp.int32, y.shape, 1)
    rot = jnp.where(lane % 32 < 16, -pltpu.roll(y, LANE - 16, 1), pltpu.roll(y, 16, 1))
    return y * cos + rot * sin


def _half_lane_ms(x, low):
    sq = x * x
    return jnp.where(low, jnp.sum(jnp.where(low, sq, 0.0), axis=-1, keepdims=True),
                     jnp.sum(jnp.where(low, 0.0, sq), axis=-1, keepdims=True)) * (1.0 / HEAD_DIM)


def _front_kernel(x_ref, xp_ref, xn_ref, g_ref, wrw_ref, wat_ref, wml_ref, wg_ref, cos_ref, sin_ref, qn_ref, kn_ref,
                  mu_ref, w0_ref, w2_ref, a0_ref, a2_ref, g2_ref, kk_ref, ka_ref, bd_ref, cw_ref, rk_ref,
                  sh_ref, df_ref, db_ref, gate_ref, bonus_ref, mq_ref, mk_ref, mvo_ref, gc_ref, qo_ref, ko_ref, va_ref, vb_ref):
    j = pl.program_id(1)
    first, last = j == 0, j == pl.num_programs(1) - 1
    g1 = g_ref[...]
    xb = _rms(x_ref[...], g1).astype(BF16)
    xb_ext = jnp.concatenate([_rms(xp_ref[...], g1).astype(BF16), xb, _rms(xn_ref[...], g1).astype(BF16)], axis=0)

    rw_ext = jnp.dot(xb_ext, wrw_ref[...], preferred_element_type=F32)
    ml_ext = jnp.dot(xb_ext, wml_ref[:, 0:2 * ML_W], preferred_element_type=F32)
    at = jnp.dot(xb, wat_ref[...], preferred_element_type=F32)
    mvo_ref[...] = jnp.dot(xb, wml_ref[:, 2 * ML_W:4 * ML_W], preferred_element_type=F32)
    gc_ref[...] = jnp.dot(xb, wg_ref[...], preferred_element_type=F32)

    x, xp, xn = _neighbours(rw_ext, first, last)
    xs = x + (0.5 * (xp + xn) - x) * mu_ref[...]
    r, k, v = xs[:, 0:RW_W], xs[:, RW_W:2 * RW_W], xs[:, 2 * RW_W:3 * RW_W]
    wd = xs[:, 3 * RW_W:3 * RW_W + LANE]
    ad = xs[:, 3 * RW_W + LANE:3 * RW_W + 2 * LANE]
    gd = xs[:, 3 * RW_W + 2 * LANE:3 * RW_W + 3 * LANE]
    kk = k * kk_ref[...]
    kap = kk / jnp.maximum(jnp.sqrt(_dot_sel(kk * kk, bd_ref[...])), 1e-12)
    sh_ref[:, 0:RW_W] = r
    sh_ref[:, RW_W:2 * RW_W] = v
    sh_ref[:, 2 * RW_W:3 * RW_W] = kap
    gate_ref[...] = _dot(_sigmoid(gd), g2_ref[...])
    tw = jnp.tanh(wd)
    kd_sum = 0.0
    for d, o_ref in ((0, df_ref), (1, db_ref)):
        wl = _dot(tw, w2_ref[d])
        o_ref[:, 0:RW_W] = -jnp.exp(-_softplus(-(w0_ref[d:d + 1, :] + wl)) - 0.5)
        a = _sigmoid(a0_ref[d:d + 1, :] + _dot(ad, a2_ref[d]))
        kd = k * (1.0 + (a - 1.0) * ka_ref[...])
        o_ref[:, RW_W:2 * RW_W] = kd
        o_ref[:, 2 * RW_W:3 * RW_W] = a
        kd_sum = kd_sum + kd
    bonus_ref[...] = _dot_sel(r * kd_sum * rk_ref[...], bd_ref[...]) * v

    x, xp, xn = _neighbours(ml_ext, first, last)
    y = cw_ref[0:1, :] * xp + cw_ref[1:2, :] * x + cw_ref[2:3, :] * xn
    y = y * _sigmoid(y)
    mq_ref[...] = y[:, 0:ML_W]
    mk_ref[...] = y[:, ML_W:2 * ML_W] * (HEAD_DIM ** -0.5)

    cos, sin = cos_ref[...], sin_ref[...]
    low = lax.broadcasted_iota(jnp.int32, (at.shape[0], LANE), 1) < HEAD_DIM
    for pair in range(AT_HEADS // 2):
        x = at[:, pair * LANE:(pair + 1) * LANE]
        y = _rope(x * lax.rsqrt(_half_lane_ms(x, low) + NORM_EPS) * qn_ref[...], cos, sin) * Q_SCALE
        swapped = pltpu.roll(y, HEAD_DIM, 1)
        if 2 * pair < AT_GROUP:
            even, odd = jnp.where(low, y, 0.0), jnp.where(low, swapped, 0.0)
        else:
            even, odd = jnp.where(low, 0.0, swapped), jnp.where(low, 0.0, y)
        qo_ref[:, 2 * pair * LANE:(2 * pair + 1) * LANE] = even.astype(BF16)
        qo_ref[:, (2 * pair + 1) * LANE:(2 * pair + 2) * LANE] = odd.astype(BF16)
    kx = at[:, AT_W:AT_W + LANE]
    ko_ref[...] = _rope(kx * lax.rsqrt(_half_lane_ms(kx, low) + NORM_EPS) * kn_ref[...], cos, sin).astype(BF16)
    v = at[:, AT_W + LANE:AT_W + 2 * LANE]
    va_ref[...] = jnp.where(low, v, 1.0).astype(BF16)
    vb_ref[...] = jnp.where(low, 1.0, v).astype(BF16)


def _front(xf, p, cos, sin, B, S, tm):
    T = B * S
    n = S // tm
    per = tm // HALO
    f32_w = (3 * RW_W, 3 * RW_W, 3 * RW_W, RW_W, RW_W, ML_W, ML_W, 2 * ML_W, LANE)
    bf16_w = (AT_HEADS * LANE, LANE, LANE, LANE)
    tok = lambda w: pl.BlockSpec((tm, w), lambda b, j: (b * n + j, 0))
    before = pl.BlockSpec((HALO, D_MODEL), lambda b, j: (jnp.maximum((b * n + j) * per - 1, 0), 0))
    after = pl.BlockSpec((HALO, D_MODEL), lambda b, j: (jnp.minimum((b * n + j + 1) * per, T // HALO - 1), 0))
    pos = pl.BlockSpec((tm, LANE), lambda b, j: (j, 0))
    return pl.pallas_call(
        _front_kernel,
        grid=(B, n),
        in_specs=[tok(D_MODEL), before, after, _const_spec((1, D_MODEL)), _const_spec((D_MODEL, RW_COLS)),
                  _const_spec((D_MODEL, AT_COLS)), _const_spec((D_MODEL, 4 * ML_W)), _const_spec((D_MODEL, LANE)),
                  pos, pos, _const_spec((1, LANE)), _const_spec((1, LANE)),
                  _const_spec((1, RW_COLS)), _const_spec((2, RW_W)), _const_spec((2, LANE, RW_W)),
                  _const_spec((2, RW_W)), _const_spec((2, LANE, RW_W)), _const_spec((GATE_LORA, RW_W)),
                  _const_spec((1, RW_W)), _const_spec((1, RW_W)), _const_spec((RW_W, RW_W)),
                  _const_spec((3, 2 * ML_W)), _const_spec((1, RW_W))],
        out_specs=[tok(w) for w in f32_w + bf16_w],
        out_shape=[jax.ShapeDtypeStruct((T, w), F32) for w in f32_w]
        + [jax.ShapeDtypeStruct((T, w), BF16) for w in bf16_w],
        compiler_params=_params(("parallel", "parallel")),
        name="front",
    )(xf, xf, xf, p["g1"], p["w_rw"], p["w_at"], p["w_ml"], p["w_g"], cos, sin, p["qn"], p["kn"],
      p["mu"], p["w0"], p["w2"], p["a0"], p["a2"], p["g2"], p["kk"], p["ka"], p["bd"], p["conv"], p["rk"])


def _head_blockdiag(y):
    lane = lax.broadcasted_iota(jnp.int32, y.shape, 1) // HEAD_DIM
    return jnp.concatenate([jnp.where(lane == h, y, 0.0) for h in range(4)], axis=0)


def _hmm(x, y):
    return _dot(x, _head_blockdiag(y))


def _hmm2(x1, x2, y):
    both = _hmm(jnp.concatenate([x1, x2], axis=0), y)
    return both[0:x1.shape[0]], both[x1.shape[0]:]


def _head_diag_blocks(f):
    lane = lax.broadcasted_iota(jnp.int32, (HEAD_DIM, 4 * HEAD_DIM), 1) // HEAD_DIM
    out = jnp.where(lane == 0, f[0:HEAD_DIM, :], 0.0)
    for h in range(1, 4):
        out = out + jnp.where(lane == h, f[h * HEAD_DIM:(h + 1) * HEAD_DIM, :], 0.0)
    return out


def _in_lockstep(gens):
    out = [None] * len(gens)
    live = list(enumerate(gens))
    while live:
        still = []
        for i, g in live:
            try:
                next(g)
                still.append((i, g))
            except StopIteration as done:
                out[i] = done.value
        live = still
    return out


def _rw_chunk(sh, dd, rev):
    L = CHUNK
    r, v, kap = sh[:, 0:RW_W], sh[:, RW_W:2 * RW_W], sh[:, 2 * RW_W:3 * RW_W]
    lw, kd, a = dd[:, 0:RW_W], dd[:, RW_W:2 * RW_W], dd[:, 2 * RW_W:3 * RW_W]
    t_i = lax.broadcasted_iota(jnp.int32, (L, RW_W), 0)
    s_i = lax.broadcasted_iota(jnp.int32, (L, RW_W), 1) % L
    tt = lax.broadcasted_iota(jnp.int32, (L, L), 0)
    ss = lax.broadcasted_iota(jnp.int32, (L, L), 1)
    if rev:
        tri = (ss >= tt).astype(BF16)
        strict, incl = s_i > t_i, s_i >= t_i
    else:
        tri = (ss <= tt).astype(BF16)
        strict, incl = s_i < t_i, s_i <= t_i
    cs = _sel_dot(tri, lw)
    yield
    last = cs[0:1, :] if rev else cs[L - 1:L, :]
    w_in, w_inv, w_ex, w_rem = jnp.exp(cs), jnp.exp(-cs), jnp.exp(cs - lw), jnp.exp(last - cs)
    b = kap * a
    rt, kt, bt, kb = r * w_in, kd * w_inv, b * w_inv, kap * w_ex
    bh, kh = b * w_rem, kd * w_rem
    gram = _dot_nt(jnp.concatenate([kb, rt], axis=0),
                   jnp.concatenate([_head_blockdiag(bt), _head_blockdiag(kt)], axis=0))
    yield
    a_b = jnp.where(strict, gram[0:L, 0:RW_W], 0.0)
    a_k = jnp.where(strict, gram[0:L, RW_W:2 * RW_W], 0.0)
    g_b = jnp.where(incl, gram[L:2 * L, 0:RW_W], 0.0)
    g_k = jnp.where(incl, gram[L:2 * L, RW_W:2 * RW_W], 0.0)
    blk16 = (t_i // 16) == (s_i // 16)
    blk32 = (t_i // 32) == (s_i // 32)
    eye = jnp.where(t_i == s_i, 1.0, 0.0)
    n1 = jnp.where(blk16, -a_b, 0.0)
    n2 = _hmm(n1, n1)
    akv, gkv = _hmm2(a_k, g_k, v)
    yield
    n4, n3 = _hmm2(n2, n1, n2)
    x = eye + n1 + n2 + n3
    yield
    n8, xn4 = _hmm2(n4, x, n4)
    x = x + xn4
    yield
    x = x + _hmm(x, n8)
    yield
    e = _hmm(x, jnp.where(blk32 & jnp.logical_not(blk16), a_b, 0.0))
    yield
    x = x - _hmm(e, x)
    yield
    e = _hmm(x, jnp.where(blk32, 0.0, a_b))
    yield
    tinv = x - _hmm(e, x)
    yield
    p1 = _hmm(tinv, kb)
    p2 = _hmm(tinv, akv)
    yield
    q = rt - _hmm(g_b, p1)
    z = gkv - _hmm(g_b, p2)
    row = lax.broadcasted_iota(jnp.int32, (HEAD_DIM, RW_W), 0)
    col = lax.broadcasted_iota(jnp.int32, (HEAD_DIM, RW_W), 1) % HEAD_DIM
    m_t = jnp.where(row == col, jnp.exp(last), 0.0) - _head_diag_blocks(_dot_tn(bh, p1))
    n_t = _head_diag_blocks(_dot_tn(jnp.concatenate([kh, -bh], axis=0), jnp.concatenate([v, p2], axis=0)))
    return jnp.concatenate([q, m_t], axis=0), z, n_t


def _rwscan_kernel(shf_ref, df_ref, shb_ref, db_ref, yf_ref, yb_ref, hf_scr, hb_scr, *, cps):
    @pl.when(pl.program_id(1) == 0)
    def _():
        hf_scr[...] = jnp.zeros_like(hf_scr)
        hb_scr[...] = jnp.zeros_like(hb_scr)

    fwd_lo = [ci * CHUNK for ci in range(cps)]
    bwd_lo = fwd_lo[::-1]
    local = _in_lockstep(
        [_rw_chunk(shf_ref[lo:lo + CHUNK, :], df_ref[lo:lo + CHUNK, :], False) for lo in fwd_lo]
        + [_rw_chunk(shb_ref[lo:lo + CHUNK, :], db_ref[lo:lo + CHUNK, :], True) for lo in bwd_lo])
    state = [hf_scr[...], hb_scr[...]]
    for ci in range(cps):
        for d, (los, y_ref) in enumerate(((fwd_lo, yf_ref), (bwd_lo, yb_ref))):
            qm, z, n_t = local[d * cps + ci]
            both = _hmm(qm, state[d])
            y_ref[los[ci]:los[ci] + CHUNK, :] = both[0:CHUNK] + z
            state[d] = both[CHUNK:2 * CHUNK] + n_t
    hf_scr[...], hb_scr[...] = state


def _rw_scan(sh, df, db, B, S, cps):
    T = B * S
    rows = cps * CHUNK
    n = S // rows
    fwd = lambda b, c: (b * n + c, 0)
    bwd = lambda b, c: (b * n + n - 1 - c, 0)
    return pl.pallas_call(
        functools.partial(_rwscan_kernel, cps=cps),
        grid=(B, n),
        in_specs=[pl.BlockSpec((rows, 3 * RW_W), fwd), pl.BlockSpec((rows, 3 * RW_W), fwd),
                  pl.BlockSpec((rows, 3 * RW_W), bwd), pl.BlockSpec((rows, 3 * RW_W), bwd)],
        out_specs=[pl.BlockSpec((rows, RW_W), fwd), pl.BlockSpec((rows, RW_W), bwd)],
        out_shape=[jax.ShapeDtypeStruct((T, RW_W), F32)] * 2,
        scratch_shapes=[pltpu.VMEM((HEAD_DIM, RW_W), F32)] * 2,
        compiler_params=_params(("parallel", "arbitrary")),
        name="rw_scan",
    )(sh, df, sh, db)


def _attn_kernel(q_ref, k_ref, va_ref, vb_ref, o_ref, s_scr, *, tq, tk, n_k):
    q = jnp.concatenate([q_ref[:, h * LANE:(h + 1) * LANE] for h in range(AT_HEADS)], axis=0)
    m_rows = AT_HEADS * tq
    half = m_rows // 2

    def scores(c):
        return lax.dot_general(q, k_ref[c * tk:(c + 1) * tk, :], (((1,), (1,)), ((), ())),
                               preferred_element_type=F32)

    def update(c, s, carry):
        m, acc = carry
        m_new = jnp.maximum(m, jnp.max(s, axis=-1, keepdims=True))
        p = jnp.exp2((s - m_new).astype(BF16))
        keys = slice(c * tk, (c + 1) * tk)
        pv = jnp.concatenate([jnp.dot(p[0:half], va_ref[keys, :], preferred_element_type=F32),
                              jnp.dot(p[half:m_rows], vb_ref[keys, :], preferred_element_type=F32)], axis=0)
        return m_new, jnp.exp2(m - m_new) * acc + pv

    carry = (jnp.full((m_rows, 1), -jnp.inf, F32), jnp.zeros((m_rows, LANE), F32))
    s_scr[0] = scores(0)
    for c in range(n_k):
        if c + 1 < n_k:
            s_scr[(c + 1) % 2] = scores(c + 1)
        carry = update(c, s_scr[c % 2], carry)
    acc = carry[1]
    o = acc / pltpu.roll(acc, HEAD_DIM, 1)
    low = lax.broadcasted_iota(jnp.int32, (tq, LANE), 1) < HEAD_DIM
    for pair in range(AT_HEADS // 2):
        even, odd = o[2 * pair * tq:(2 * pair + 1) * tq, :], o[(2 * pair + 1) * tq:(2 * pair + 2) * tq, :]
        if 2 * pair < AT_GROUP:
            both = jnp.where(low, even, pltpu.roll(odd, HEAD_DIM, 1))
        else:
            both = jnp.where(low, pltpu.roll(even, HEAD_DIM, 1), odd)
        o_ref[:, pair * LANE:(pair + 1) * LANE] = both.astype(BF16)


def _attn(qh, kh, va, vb, B, S, tq, tk):
    T = B * S
    n = S // tq
    seq = pl.BlockSpec((S, LANE), lambda b, i: (b, 0))
    return pl.pallas_call(
        functools.partial(_attn_kernel, tq=tq, tk=tk, n_k=S // tk),
        grid=(B, n),
        in_specs=[pl.BlockSpec((tq, AT_HEADS * LANE), lambda b, i: (b * n + i, 0)),
                  seq, seq, seq],
        out_specs=pl.BlockSpec((tq, AT_W), lambda b, i: (b * n + i, 0)),
        out_shape=jax.ShapeDtypeStruct((T, AT_W), BF16),
        scratch_shapes=[pltpu.VMEM((2, AT_HEADS * tq, tk), F32)],
        compiler_params=_params(("parallel", "parallel")),
        name="attn",
    )(qh, kh, va, vb)


def _scan_max(x, rev):
    n = x.shape[0]
    row = lax.broadcasted_iota(jnp.int32, x.shape, 0)
    step = 1
    while step < n:
        if rev:
            shifted = jnp.where(row >= n - step, -jnp.inf, pltpu.roll(x, n - step, 0))
        else:
            shifted = jnp.where(row < step, -jnp.inf, pltpu.roll(x, step, 0))
        x = jnp.maximum(x, shifted)
        step *= 2
    return x


def _ml_chunk_local(q, k, v, ig, lf, bd, rev):
    L = CHUNK
    t_i = lax.broadcasted_iota(jnp.int32, (L, ML_W), 0)
    s_i = lax.broadcasted_iota(jnp.int32, (L, ML_W), 1) % L
    tt = lax.broadcasted_iota(jnp.int32, (L, L), 0)
    ss = lax.broadcasted_iota(jnp.int32, (L, L), 1)
    tri = ((ss >= tt) if rev else (ss <= tt)).astype(BF16)
    causal = (s_i >= t_i) if rev else (s_i <= t_i)
    bcol = _sel_dot(tri, lf)
    qk = _dot_nt(q, _head_blockdiag(k))
    yield
    b_last = bcol[0:1, :] if rev else bcol[L - 1:L, :]
    c = ig - bcol
    m_loc = bcol + _scan_max(c, rev)
    c_row = jnp.sum(jnp.where(t_i == s_i, c, 0.0), axis=0, keepdims=True)
    pb = (jnp.exp(jnp.where(causal, bcol + c_row - m_loc, -jnp.inf)) * qk).astype(BF16)
    g = b_last - bcol + ig
    mg = jnp.max(g, axis=0, keepdims=True)
    wgt = jnp.exp(g - mg)
    num = jnp.dot(pb, _head_blockdiag(v).astype(BF16), preferred_element_type=F32)
    den = jnp.dot(pb, bd, preferred_element_type=F32)
    kc = _head_diag_blocks(_dot_tn(k, wgt * v))
    nc = jnp.sum(wgt * k, axis=0, keepdims=True)
    return dict(q=q, bcol=bcol, b_last=b_last, m_loc=m_loc, num=num, den=den, mg=mg, kc=kc, nc=nc)


def _ml_chunk_combine(loc, ct, n, m, bd):
    q = loc["q"]
    inter = loc["bcol"] + m
    m_t = jnp.maximum(loc["m_loc"], inter)
    e_loc, e_int = jnp.exp(loc["m_loc"] - m_t), jnp.exp(inter - m_t)
    num = e_loc * loc["num"] + e_int * _hmm(q, ct)
    den = e_loc * loc["den"] + e_int * jnp.dot((q * n).astype(BF16), bd, preferred_element_type=F32)
    h = num / jnp.maximum(jnp.abs(den), jnp.exp(-m_t))
    m_new = jnp.maximum(loc["b_last"] + m, loc["mg"])
    a1 = jnp.exp(loc["b_last"] + m - m_new)
    a2 = jnp.exp(loc["mg"] - m_new)
    return h, a1 * ct + a2 * loc["kc"], a1 * n + a2 * loc["nc"], m_new


def _mlscan_kernel(qf_ref, kf_ref, vf_ref, gf_ref, qb_ref, kb_ref, vb_ref, gb_ref, bias_ref, ex_ref, bd_ref,
                   hf_ref, hb_ref, ct_scr, n_scr, m_scr, *, cps):
    @pl.when(pl.program_id(1) == 0)
    def _():
        ct_scr[...] = jnp.zeros_like(ct_scr)
        n_scr[...] = jnp.zeros_like(n_scr)
        m_scr[...] = jnp.zeros_like(m_scr)

    bd = bd_ref[...]
    dirs = ((0, False, qf_ref, kf_ref, vf_ref, gf_ref, hf_ref), (1, True, qb_ref, kb_ref, vb_ref, gb_ref, hb_ref))
    gens = []
    for d, rev, q_ref, k_ref, v_ref, g_ref, h_ref in dirs:
        gates = g_ref[...] + bias_ref[...]
        log_sig = jnp.minimum(gates, 0.0) - jnp.log1p(jnp.exp(-jnp.abs(gates)))
        lane = lax.broadcasted_iota(jnp.int32, gates.shape, 1)
        spread = _dot_sel(jnp.where(lane < 2 * ML_HEADS, gates, log_sig), ex_ref[d])
        order = [(cps - 1 - ci) * CHUNK for ci in range(cps)] if rev else [ci * CHUNK for ci in range(cps)]
        gens += [_ml_chunk_local(q_ref[lo:lo + CHUNK, :], k_ref[lo:lo + CHUNK, :], v_ref[lo:lo + CHUNK, :],
                                 spread[lo:lo + CHUNK, 0:ML_W], spread[lo:lo + CHUNK, ML_W:2 * ML_W], bd, rev)
                 for lo in order]
    local = _in_lockstep(gens)
    state = [(ct_scr[d], n_scr[d], m_scr[d]) for d in range(2)]
    for ci in range(cps):
        for d, rev, _, _, _, _, h_ref in dirs:
            lo = (cps - 1 - ci) * CHUNK if rev else ci * CHUNK
            out, *state[d] = _ml_chunk_combine(local[d * cps + ci], *state[d], bd)
            h_ref[lo:lo + CHUNK, :] = out
    for d in range(2):
        ct_scr[d], n_scr[d], m_scr[d] = state[d]


def _ml_scan(mq, mk, ml, gc, bias, ex, bd, B, S, cps):
    T = B * S
    rows = cps * CHUNK
    n = S // rows
    fwd = lambda b, c: (b * n + c, 0)
    bwd = lambda b, c: (b * n + n - 1 - c, 0)
    fwd_v, bwd_v = fwd, bwd
    tile = lambda im: pl.BlockSpec((rows, ML_W), im)
    gate = lambda im: pl.BlockSpec((rows, LANE), im)
    return pl.pallas_call(
        functools.partial(_mlscan_kernel, cps=cps),
        grid=(B, n),
        in_specs=[tile(fwd), tile(fwd), tile(fwd_v), gate(fwd), tile(bwd), tile(bwd), tile(bwd_v), gate(bwd),
                  _const_spec((1, LANE)), _const_spec((2, LANE, 2 * ML_W)), _const_spec((ML_W, ML_W))],
        out_specs=[tile(fwd), tile(bwd)],
        out_shape=[jax.ShapeDtypeStruct((T, ML_W), F32)] * 2,
        scratch_shapes=[pltpu.VMEM((2, HEAD_DIM, ML_W), F32), pltpu.VMEM((2, 1, ML_W), F32),
                        pltpu.VMEM((2, 1, ML_W), F32)],
        compiler_params=_params(("parallel", "arbitrary")),
        name="ml_scan",
    )(mq, mk, ml, gc, mq, mk, ml, gc, bias, ex, bd)


def _back_kernel(x_ref, yf_ref, yb_ref, bonus_ref, g_ref, ao_ref, hf_ref, hb_ref, og_ref,
                 lnw_ref, lnb_ref, nw_ref, bd_ref, worw_ref, woat_ref, woml_ref,
                 g2_ref, w1_ref, w2_ref, gf_ref, o_ref, *, final):
    bd = bd_ref[...]
    inv = 1.0 / HEAD_DIM
    h = x_ref[...] + jnp.dot(ao_ref[...], woat_ref[...], preferred_element_type=F32)
    wkv = yf_ref[...] + yb_ref[...]
    hm = hf_ref[...] + hb_ref[...]
    mean = _dot_sel(wkv, bd) * inv
    hms = _dot_sel(hm * hm, bd) * inv
    y_ml = _sigmoid(og_ref[...]) * (hm * lax.rsqrt(hms + NORM_EPS) * nw_ref[...])
    h = h + _dot(y_ml, woml_ref[...])
    dev = wkv - mean
    var = _dot_sel(dev * dev, bd) * inv
    y = dev * lax.rsqrt(var + RW_LN_EPS) * lnw_ref[...] + lnb_ref[...]
    h = h + _dot((y + bonus_ref[...]) * g_ref[...], worw_ref[...])
    u = jnp.dot(_rms(h, g2_ref[...]).astype(BF16), w1_ref[...], preferred_element_type=F32)
    u = jnp.square(jnp.maximum(u, 0.0)).astype(BF16)
    out = h + jnp.dot(u, w2_ref[...], preferred_element_type=F32)
    if final:
        out = _rms(out, gf_ref[...])
    o_ref[...] = out


def _back(xf, yf, yb, bonus, gate, ao, hf, hb, mvo, p, gf, final, tm):
    T = xf.shape[0]
    tok = lambda w, c=0: pl.BlockSpec((tm, w), lambda i: (i, c))
    vec = lambda w: _const_spec((1, w))
    once = lambda shape: pl.BlockSpec(shape, lambda i: (0, 0), pipeline_mode=pl.Buffered(1))
    return pl.pallas_call(
        functools.partial(_back_kernel, final=final),
        grid=(T // tm,),
        in_specs=[tok(D_MODEL), tok(RW_W), tok(RW_W), tok(RW_W), tok(RW_W), tok(AT_W), tok(ML_W), tok(ML_W),
                  tok(ML_W, 1), vec(RW_W), vec(RW_W), vec(ML_W), once((RW_W, RW_W)),
                  once((RW_W, D_MODEL)), once((AT_W, D_MODEL)), once((ML_W, D_MODEL)),
                  vec(D_MODEL), once((D_MODEL, D_FF)), once((D_FF, D_MODEL)), vec(D_MODEL)],
        out_specs=tok(D_MODEL),
        out_shape=jax.ShapeDtypeStruct((T, D_MODEL), F32),
        compiler_params=_params(("parallel",)),
        name="back_final" if final else "back",
    )(xf, yf, yb, bonus, gate, ao, hf, hb, mvo, p["lnw"], p["lnb"], p["nw"], p["bd"],
      p["wo_rw"], p["wo_at"], p["wo_ml"], p["g2n"], p["w1"], p["w2m"], gf)


def _rope_tables(seq_len):
    t = np.arange(seq_len)
    n_freq = HEAD_DIM // 4
    inv = ROPE_THETA ** (-jnp.arange(n_freq, dtype=F32) / n_freq)
    ang_r = jnp.asarray(t // GRID_W, F32)[:, None] * inv
    ang_c = jnp.asarray(t % GRID_W, F32)[:, None] * inv
    ang = jnp.concatenate([ang_r, ang_r, ang_c, ang_c] * 2, axis=-1)
    return jnp.cos(ang), jnp.sin(ang)


def _layer_params(l, norm1_g, w_in, rw_mu, rw_w0, rw_w2, rw_a0, rw_a2, rw_g2, rw_kk, rw_ka, rw_rk, rw_lnw, rw_lnb,
                  at_qn, at_kn, ml_conv, ml_ib, ml_fb, ml_nw, w_out, norm2_g, mlp_w1, mlp_w2):
    w = w_in[l]
    w_at = w[:, RW_COLS:RW_COLS + AT_COLS]
    w_ml = w[:, RW_COLS + AT_COLS:]
    zero_lora = jnp.zeros((DECAY_LORA, RW_W), F32)
    lora = lambda m: jnp.stack([jnp.concatenate([m[0], zero_lora], 0), jnp.concatenate([zero_lora, m[1]], 0)])
    idx = np.arange(RW_W) // HEAD_DIM
    gate_col = np.arange(LANE)[:, None]
    spread = lambda first: gate_col == first + idx[None, :]
    ex = np.stack([np.concatenate([spread(d * ML_HEADS), spread(2 * ML_HEADS + d * ML_HEADS)], axis=1)
                   for d in range(2)])
    wo = w_out[l]
    return dict(
        g1=norm1_g[l][None, :],
        w_rw=w[:, :RW_COLS].astype(BF16),
        w_at=w_at.astype(BF16),
        w_ml=w_ml[:, :4 * ML_W].astype(BF16),
        w_g=jnp.pad(w_ml[:, 4 * ML_W:], ((0, 0), (0, LANE - 4 * ML_HEADS))).astype(BF16),
        mu=rw_mu[l][None, :], w0=rw_w0[l], w2=lora(rw_w2[l]), a0=rw_a0[l], a2=lora(rw_a2[l]), g2=rw_g2[l],
        kk=rw_kk[l][None, :], ka=rw_ka[l][None, :],
        bd=jnp.asarray(idx[:, None] == idx[None, :], BF16),
        rk=rw_rk[l].reshape(1, RW_W), lnw=rw_lnw[l][None, :], lnb=rw_lnb[l][None, :],
        qn=jnp.tile(at_qn[l], 2)[None, :], kn=jnp.tile(at_kn[l], 2)[None, :],
        conv=ml_conv[l],
        gbias=jnp.pad(jnp.concatenate([ml_ib[l].reshape(-1), ml_fb[l].reshape(-1)]),
                      (0, LANE - 4 * ML_HEADS))[None, :],
        nw=ml_nw[l][None, :], ex=jnp.asarray(ex, BF16),
        wo_rw=wo[:RW_W].astype(BF16),
        wo_at=wo[RW_W:RW_W + AT_W].astype(BF16),
        wo_ml=wo[RW_W + AT_W:].astype(BF16),
        g2n=norm2_g[l][None, :], w1=mlp_w1[l].astype(BF16), w2m=mlp_w2[l].astype(BF16),
    )


def _tiles(S):
    return dict(tm=512, cps=8, tq=256, tk=1024)


def _trunk(x, layers, final_g):
    B, S, _ = x.shape
    t = _tiles(S)
    xf = x.reshape(B * S, D_MODEL)
    cos, sin = _rope_tables(S)
    gf = final_g[None, :]
    for l, p in enumerate(layers):
        sh, df, db, gate, bonus, mq, mk, mvo, gc, qh, kh, va, vb = _front(xf, p, cos, sin, B, S, t["tm"])
        yf, yb = _rw_scan(sh, df, db, B, S, t["cps"])
        ao = _attn(qh, kh, va, vb, B, S, t["tq"], t["tk"])
        hf, hb = _ml_scan(mq, mk, mvo, gc, p["gbias"], p["ex"], p["bd"], B, S, t["cps"])
        xf = _back(xf, yf, yb, bonus, gate, ao, hf, hb, mvo, p, gf, l == len(layers) - 1, t["tm"])
    return xf.reshape(B, S, D_MODEL)


def kernel(x_prompt, x_sample, norm1_g, w_in, rw_mu, rw_w0, rw_w2, rw_a0, rw_a2, rw_g2, rw_kk, rw_ka, rw_rk,
           rw_lnw, rw_lnb, at_qn, at_kn, ml_conv, ml_ib, ml_fb, ml_nw, w_out, norm2_g, mlp_w1, mlp_w2, final_g):
    layers = [_layer_params(l, norm1_g, w_in, rw_mu, rw_w0, rw_w2, rw_a0, rw_a2, rw_g2, rw_kk, rw_ka, rw_rk,
                            rw_lnw, rw_lnb, at_qn, at_kn, ml_conv, ml_ib, ml_fb, ml_nw, w_out, norm2_g,
                            mlp_w1, mlp_w2) for l in range(DEPTH)]
    return _trunk(x_prompt, layers, final_g), _trunk(x_sample, layers, final_g)
```
